```python
import jax, jax.numpy as jnp
from jax import lax
import numpy as np

D_MODEL = 1024
BATCH = 8
SEQ = 2048
DEPTH = 4

ATT_HEADS = 16
ATT_KV_HEADS = 4
ATT_GROUP = ATT_HEADS // ATT_KV_HEADS
ATT_HEAD_DIM = 64
WINDOW = 128
DN_HEADS = 8
DN_HEAD_DIM = 128
CONV_K = 4
CHUNK = 64
D_FF = 4 * D_MODEL
ATT_Q_W = ATT_HEADS * ATT_HEAD_DIM
ATT_KV_W = ATT_KV_HEADS * ATT_HEAD_DIM
DN_W = DN_HEADS * DN_HEAD_DIM
IN_SPLITS = (ATT_Q_W, ATT_KV_W, ATT_KV_W, DN_W, DN_W, DN_W, DN_W, DN_HEADS, DN_HEADS, D_MODEL, D_MODEL)
D_IN = ATT_Q_W + 2 * ATT_KV_W + 4 * DN_W + 2 * DN_HEADS + 2 * D_MODEL
ALPHA = (2 * DEPTH) ** 0.25
BETA_INIT = (8 * DEPTH) ** -0.25
LN_EPS = 1e-5
RMS_EPS = 1e-6
ADA_SCALE = 0.2

kernel_name = "hybrid_swa_sink_gdn_parallel_deepnorm_adaln"


def _split_in(p):
    outs = []
    off = 0
    for w in IN_SPLITS:
        outs.append(p[..., off:off + w])
        off += w
    return outs


def _layer_norm(x, g, b):
    xf = x.astype(jnp.float32)
    mu = jnp.mean(xf, axis=-1, keepdims=True)
    var = jnp.mean(jnp.square(xf - mu), axis=-1, keepdims=True)
    return ((xf - mu) * lax.rsqrt(var + LN_EPS) * g.astype(jnp.float32) + b.astype(jnp.float32)).astype(x.dtype)


def _l2norm(t):
    return t * lax.rsqrt(jnp.sum(jnp.square(t), axis=-1, keepdims=True) + RMS_EPS)


def _causal_conv_silu(x, w):
    s = x.shape[1]
    xp = jnp.pad(x, ((0, 0), (CONV_K - 1, 0), (0, 0)))
    y = sum(xp[:, j:j + s] * w[j] for j in range(CONV_K))
    return jax.nn.silu(y)


def _sliding_window_attention(q, k, v, sinks):
    b, s, _ = q.shape
    nb = s // WINDOW
    qb = q.reshape(b, nb, WINDOW, ATT_KV_HEADS, ATT_GROUP, ATT_HEAD_DIM)

    def band(t):
        tp = jnp.pad(t, ((0, 0), (WINDOW, 0), (0, 0)))
        tb = tp.reshape(b, nb + 1, WINDOW, ATT_KV_HEADS, ATT_HEAD_DIM)
        return jnp.concatenate([tb[:, :-1], tb[:, 1:]], axis=2)

    kb, vb = band(k), band(v)
    scores = jnp.einsum('bnqhgd,bnshd->bnhgqs', qb, kb).astype(jnp.float32) * (ATT_HEAD_DIM ** -0.5)
    qi = jnp.arange(WINDOW)[:, None]
    si = jnp.arange(2 * WINDOW)[None, :]
    diff = qi + WINDOW - si
    blk = jnp.arange(nb)[:, None, None]
    valid = (diff >= 0) & (diff < WINDOW) & (blk * WINDOW + si - WINDOW >= 0)
    scores = jnp.where(valid[None, :, None, None], scores, -jnp.inf)
    sink = sinks.astype(jnp.float32).reshape(1, 1, ATT_KV_HEADS, ATT_GROUP, 1, 1)
    m = jnp.maximum(jnp.max(scores, axis=-1, keepdims=True), sink)
    p = jnp.exp(scores - m)
    denom = jnp.sum(p, axis=-1, keepdims=True) + jnp.exp(sink - m)
    probs = (p / denom).astype(v.dtype)
    o = jnp.einsum('bnhgqs,bnshd->bnqhgd', probs, vb)
    return o.reshape(b, s, ATT_Q_W)


def _gated_delta_rule(q, k, v, beta, g):
    b, s, h, d = q.shape
    n = s // CHUNK

    def chunks(t):
        t = t.reshape((b, n, CHUNK, h) + t.shape[3:])
        return jnp.moveaxis(t, 3, 1)

    q, k, v, beta, g = (chunks(t) for t in (q, k, v, beta, g))
    g_cum = jnp.cumsum(g, axis=-1)
    causal = jnp.tril(jnp.ones((CHUNK, CHUNK), dtype=bool))
    strict = jnp.tril(jnp.ones((CHUNK, CHUNK), dtype=bool), -1)
    decay = jnp.exp(jnp.where(causal, g_cum[..., :, None] - g_cum[..., None, :], -jnp.inf))
    kb = k * beta[..., None]
    vb = v * beta[..., None]
    l_mat = jnp.where(strict, jnp.einsum('bhncd,bhnsd->bhncs', kb, k) * decay, 0.0)
    a_mat = l_mat + jnp.eye(CHUNK, dtype=l_mat.dtype)
    rhs = jnp.concatenate([vb, kb * jnp.exp(g_cum)[..., None]], axis=-1)
    sol = lax.linalg.triangular_solve(a_mat, rhs, left_side=True, lower=True, unit_diagonal=True)
    u, w = sol[..., :d], sol[..., d:]
    intra = jnp.einsum('bhncd,bhnsd->bhncs', q, k) * decay
    q_dec = q * jnp.exp(g_cum)[..., None]
    k_dec = k * jnp.exp(g_cum[..., -1:] - g_cum)[..., None]
    last = jnp.exp(g_cum[..., -1])
    xs = tuple(jnp.moveaxis(t, 2, 0) for t in (u, w, intra, q_dec, k_dec, last))

    def step(state, inp):
        u_c, w_c, intra_c, q_c, k_c, last_c = inp
        v_new = u_c - jnp.einsum('bhck,bhkv->bhcv', w_c, state)
        o_c = jnp.einsum('bhck,bhkv->bhcv', q_c, state) + jnp.einsum('bhcs,bhsv->bhcv', intra_c, v_new)
        state = state * last_c[..., None, None] + jnp.einsum('bhck,bhcv->bhkv', k_c, v_new)
        return state, o_c

    s0 = jnp.zeros((b, h, d, d), jnp.float32)
    _, o = lax.scan(step, s0, xs)
    return jnp.transpose(o, (1, 0, 3, 2, 4)).reshape(b, s, h, d)


def _gated_deltanet(dq, dk, dv, z, b_raw, a_raw, conv_w, a_log, dt_bias, norm_w):
    bsz, s, _ = dq.shape
    qkv = _causal_conv_silu(jnp.concatenate([dq, dk, dv], axis=-1), conv_w)
    shp = (bsz, s, DN_HEADS, DN_HEAD_DIM)
    q = _l2norm(qkv[..., :DN_W].reshape(shp).astype(jnp.float32)) * (DN_HEAD_DIM ** -0.5)
    k = _l2norm(qkv[..., DN_W:2 * DN_W].reshape(shp).astype(jnp.float32))
    v = qkv[..., 2 * DN_W:].reshape(shp).astype(jnp.float32)
    beta = jax.nn.sigmoid(b_raw.astype(jnp.float32))
    g = -jnp.exp(a_log.astype(jnp.float32)) * jax.nn.softplus(a_raw.astype(jnp.float32) + dt_bias.astype(jnp.float32))
    o = _gated_delta_rule(q, k, v, beta, g)
    o = o * lax.rsqrt(jnp.mean(jnp.square(o), axis=-1, keepdims=True) + RMS_EPS) * norm_w.astype(jnp.float32)
    o = o * jax.nn.silu(z.reshape(shp).astype(jnp.float32))
    return o.reshape(bsz, s, DN_W).astype(dq.dtype)


def setup_inputs(seed: int = 0) -> dict:
    key = jax.random.key(seed)
    ks = jax.random.split(key, 24)
    nrm = jax.random.normal
    L, D = DEPTH, D_MODEL
    dt = jnp.exp(jax.random.uniform(ks[8], (L, DN_HEADS), minval=np.log(1e-3), maxval=np.log(1e-1)))
    return {
        "x": nrm(ks[0], (BATCH, SEQ, D), jnp.float32),
        "c": nrm(ks[1], (BATCH, D), jnp.float32),
        "w_ada": nrm(ks[2], (L, D, 6 * D), jnp.float32) * (ADA_SCALE * D ** -0.5),
        "b_ada": nrm(ks[3], (L, 6 * D), jnp.float32) * 0.01,
        "w_in": nrm(ks[4], (L, D, D_IN), jnp.float32) * D ** -0.5,
        "conv_w": nrm(ks[5], (L, CONV_K, 3 * DN_W), jnp.float32) * CONV_K ** -0.5,
        "a_log": jnp.log(jax.random.uniform(ks[6], (L, DN_HEADS), minval=1.0, maxval=16.0)),
        "dt_bias": dt + jnp.log(-jnp.expm1(-dt)),
        "sinks": nrm(ks[7], (L, ATT_HEADS), jnp.float32),
        "dn_norm_w": 1.0 + 0.02 * nrm(ks[9], (L, DN_HEAD_DIM), jnp.float32),
        "w_oa": nrm(ks[10], (L, ATT_Q_W, D), jnp.float32) * ATT_Q_W ** -0.5,
        "w_ob": nrm(ks[11], (L, DN_W, D), jnp.float32) * DN_W ** -0.5,
        "w_out": nrm(ks[12], (L, D, D), jnp.float32) * (BETA_INIT * D ** -0.5),
        "ln1_g": 1.0 + 0.02 * nrm(ks[13], (L, D), jnp.float32),
        "ln1_b": 0.02 * nrm(ks[14], (L, D), jnp.float32),
        "w_ff1": nrm(ks[15], (L, D, D_FF), jnp.float32) * D ** -0.5,
        "b_ff1": 0.02 * nrm(ks[16], (L, D_FF), jnp.float32),
        "w_ff2": nrm(ks[17], (L, D_FF, D), jnp.float32) * (BETA_INIT * D_FF ** -0.5),
        "b_ff2": 0.02 * nrm(ks[18], (L, D), jnp.float32),
        "ln2_g": 1.0 + 0.02 * nrm(ks[19], (L, D), jnp.float32),
        "ln2_b": 0.02 * nrm(ks[20], (L, D), jnp.float32),
    }


def reference(x, c, w_ada, b_ada, w_in, conv_w, a_log, dt_bias, sinks, dn_norm_w, w_oa, w_ob, w_out,
              ln1_g, ln1_b, w_ff1, b_ff1, w_ff2, b_ff2, ln2_g, ln2_b):
    c_act = jax.nn.silu(c)
    for l in range(DEPTH):
        mod = c_act @ w_ada[l] + b_ada[l]
        sh1, sc1, gt1, sh2, sc2, gt2 = jnp.split(mod[:, None, :], 6, axis=-1)
        u = x * (1.0 + sc1) + sh1
        proj = u @ w_in[l]
        qa, ka, va, dq, dk, dv, z, b_raw, a_raw, g_a, g_b = _split_in(proj)
        y_a = _sliding_window_attention(qa, ka, va, sinks[l]) @ w_oa[l]
        y_b = _gated_deltanet(dq, dk, dv, z, b_raw, a_raw, conv_w[l], a_log[l], dt_bias[l], dn_norm_w[l]) @ w_ob[l]
        mixed = (jax.nn.sigmoid(g_a) * y_a + jax.nn.sigmoid(g_b) * y_b) @ w_out[l]
        x = _layer_norm(ALPHA * x + (1.0 + gt1) * mixed, ln1_g[l], ln1_b[l])
        u2 = x * (1.0 + sc2) + sh2
        h = jnp.square(jax.nn.relu(u2 @ w_ff1[l] + b_ff1[l]))
        x = _layer_norm(ALPHA * x + (1.0 + gt2) * (h @ w_ff2[l] + b_ff2[l]), ln2_g[l], ln2_b[l])
    return x
```

```python
import functools

import jax
import jax.numpy as jnp
from jax import lax
from jax.experimental import pallas as pl
from jax.experimental.pallas import tpu as pltpu

F32 = jnp.float32
BF16 = jnp.bfloat16

D_MODEL = 1024
DEPTH = 4
ATT_HEADS = 16
ATT_KV_HEADS = 4
ATT_GROUP = ATT_HEADS // ATT_KV_HEADS
ATT_HEAD_DIM = 64
WINDOW = 128
DN_HEADS = 8
DN_HEAD_DIM = 128
CONV_K = 4
CHUNK = 64
D_FF = 4 * D_MODEL
ATT_Q_W = ATT_HEADS * ATT_HEAD_DIM
ATT_KV_W = ATT_KV_HEADS * ATT_HEAD_DIM
DN_W = DN_HEADS * DN_HEAD_DIM
ALPHA = (2 * DEPTH) ** 0.25
LN_EPS = 1e-5
RMS_EPS = 1e-6

LANES = 128
GATE_LANES = 2 * DN_HEADS
N_MAIN = ATT_Q_W + 4 * DN_W + 2 * D_MODEL + 2 * ATT_KV_W
COLBLK_Q, COLBLK_DQ, COLBLK_DK, COLBLK_DV, COLBLK_Z, COLBLK_GA, COLBLK_GB = range(7)
COLBLK_K = (ATT_Q_W + 4 * DN_W + 2 * D_MODEL) // ATT_KV_W
COLBLK_V = COLBLK_K + 1
VMEM_LIMIT = 48 * 1024 * 1024


def _params(*sem):
    return pltpu.CompilerParams(dimension_semantics=sem, vmem_limit_bytes=VMEM_LIMIT)


def _dot(a, b):
    return jnp.dot(a.astype(BF16), b.astype(BF16), preferred_element_type=F32)


def _dot_nt(a, b):
    return lax.dot_general(a.astype(BF16), b.astype(BF16), (((1,), (1,)), ((), ())), preferred_element_type=F32)


def _dot_tn(a, b):
    return lax.dot_general(a.astype(BF16), b.astype(BF16), (((0,), (0,)), ((), ())), preferred_element_type=F32)


def _dot_f32(a, b):
    return jnp.dot(a, b, precision=lax.Precision.HIGHEST, preferred_element_type=F32)


def _silu(t):
    return t * jax.nn.sigmoid(t)


def _layer_norm(t, g, b):
    mu = jnp.mean(t, axis=-1, keepdims=True)
    tc = t - mu
    var = jnp.mean(tc * tc, axis=-1, keepdims=True)
    return tc * lax.rsqrt(var + LN_EPS) * g + b


def _ada_kernel(c_ref, w_ref, b_ref, o_ref):
    ca = _silu(c_ref[...])
    o_ref[...] = _dot(ca, w_ref[...]) + b_ref[...]


def _ada_mod(c, w_ada, b_ada):
    depth, d, n = w_ada.shape
    bsz = c.shape[0]
    tn = 1536
    return pl.pallas_call(
        _ada_kernel,
        grid=(depth, n // tn),
        in_specs=[
            pl.BlockSpec((bsz, d), lambda l, j: (0, 0)),
            pl.BlockSpec((None, d, tn), lambda l, j: (l, 0, j)),
            pl.BlockSpec((None, 1, tn), lambda l, j: (l, 0, j)),
        ],
        out_specs=pl.BlockSpec((None, bsz, tn), lambda l, j: (l, 0, j)),
        out_shape=jax.ShapeDtypeStruct((depth, bsz, n), F32),
        compiler_params=_params("parallel", "parallel"),
        name="ada_mod",
    )(c, w_ada, b_ada.reshape(depth, 1, n))


def _inproj_kernel(x_ref, mod_ref, w_ref, wg_ref, wgt_ref, proj_ref, gate_ref, gatet_ref, u_scr):
    j = pl.program_id(2)

    @pl.when(j == 0)
    def _():
        u = (x_ref[...] * (1.0 + mod_ref[1:2, :]) + mod_ref[0:1, :]).astype(BF16)
        u_scr[...] = u
        gate_ref[...] = jnp.dot(u, wg_ref[...], preferred_element_type=F32)
        for ci in range(u.shape[0] // CHUNK):
            uc = u[ci * CHUNK:(ci + 1) * CHUNK]
            gatet_ref[ci] = lax.dot_general(wgt_ref[...], uc, (((1,), (1,)), ((), ())), preferred_element_type=F32)

    proj_ref[...] = jnp.dot(u_scr[...], w_ref[...], preferred_element_type=F32).astype(BF16)


def _in_proj(x, mod_l, w_main, w_gate, w_gate_t):
    bsz, s, d = x.shape
    tm = min(1024, s)
    tn = 1536
    return pl.pallas_call(
        _inproj_kernel,
        grid=(bsz, s // tm, N_MAIN // tn),
        in_specs=[
            pl.BlockSpec((None, tm, d), lambda b, i, j: (b, i, 0)),
            pl.BlockSpec((None, 6, d), lambda b, i, j: (b, 0, 0)),
            pl.BlockSpec((d, tn), lambda b, i, j: (0, j)),
            pl.BlockSpec((d, LANES), lambda b, i, j: (0, 0)),
            pl.BlockSpec((GATE_LANES, d), lambda b, i, j: (0, 0)),
        ],
        out_specs=[
            pl.BlockSpec((None, tm, tn), lambda b, i, j: (b, i, j)),
            pl.BlockSpec((None, tm, LANES), lambda b, i, j: (b, i, 0)),
            pl.BlockSpec((None, tm // CHUNK, GATE_LANES, CHUNK), lambda b, i, j: (b, i, 0, 0)),
        ],
        out_shape=[
            jax.ShapeDtypeStruct((bsz, s, N_MAIN), BF16),
            jax.ShapeDtypeStruct((bsz, s, LANES), F32),
            jax.ShapeDtypeStruct((bsz, s // CHUNK, GATE_LANES, CHUNK), F32),
        ],
        scratch_shapes=[pltpu.VMEM((tm, d), BF16)],
        compiler_params=_params("parallel", "parallel", "arbitrary"),
        name="in_proj",
    )(x, mod_l, w_main, w_gate, w_gate_t)


def _attn_kernel(sink_ref, q_ref, kp_ref, kc_ref, vp_ref, vc_ref, o_ref, *, tq):
    i = pl.program_id(1)
    nsub = tq // WINDOW
    hd, grp, kvh = ATT_HEAD_DIM, ATT_GROUP, ATT_KV_HEADS
    kfull = jnp.concatenate([kp_ref[...], kc_ref[...]], axis=0).astype(F32)
    vfull = jnp.concatenate([vp_ref[...], vc_ref[...]], axis=0).astype(F32)
    lane = lax.broadcasted_iota(jnp.int32, kfull.shape, 1)
    k_rolled = [kfull] + [pltpu.roll(kfull, r * hd, axis=1) for r in range(1, kvh)]
    v_rolled = [vfull] + [pltpu.roll(vfull, r * hd, axis=1) for r in range(1, kvh)]

    def placed(rolled, kv, g):
        src = rolled[(g - kv) % kvh]
        return jnp.where((lane >= g * hd) & (lane < (g + 1) * hd), src, 0.0).astype(BF16)

    qi = lax.broadcasted_iota(jnp.int32, (WINDOW, 2 * WINDOW), 0)
    si = lax.broadcasted_iota(jnp.int32, (WINDOW, 2 * WINDOW), 1)
    diff = qi + WINDOW - si
    band = (diff >= 0) & (diff < WINDOW)
    band0 = band & ((si >= WINDOW) | (i > 0))
    neg_inf = jnp.float32(-jnp.inf)

    for kv in range(kvh):
        kp = [placed(k_rolled, kv, g) for g in range(grp)]
        vp = [placed(v_rolled, kv, g) for g in range(grp)]
        for s in range(nsub):
            r0, r1 = s * WINDOW, (s + 2) * WINDOW
            kbd = jnp.concatenate([kp[g][r0:r1] for g in range(grp)], axis=0)
            vbd = jnp.concatenate([vp[g][r0:r1] for g in range(grp)], axis=0)
            qs = q_ref[s * WINDOW:(s + 1) * WINDOW, kv * grp * hd:(kv + 1) * grp * hd]
            sc = lax.dot_general(qs, kbd, (((1,), (1,)), ((), ())), preferred_element_type=F32)
            sc = sc * (hd ** -0.5)
            valid = band0 if s == 0 else band
            probs = []
            for g in range(grp):
                sg = jnp.where(valid, sc[:, g * 2 * WINDOW:(g + 1) * 2 * WINDOW], neg_inf)
                sink = sink_ref[kv * grp + g]
                m = jnp.maximum(jnp.max(sg, axis=-1, keepdims=True), sink)
                p = jnp.exp(sg - m)
                denom = jnp.sum(p, axis=-1, keepdims=True) + jnp.exp(sink - m)
                probs.append((p / denom).astype(BF16))
            pcat = jnp.concatenate(probs, axis=1)
            o = jnp.dot(pcat, vbd, preferred_element_type=F32)
            o_ref[s * WINDOW:(s + 1) * WINDOW, kv * grp * hd:(kv + 1) * grp * hd] = o.astype(BF16)


def _attention(proj, sinks):
    bsz, s, _ = proj.shape
    tq = min(512, s)
    sub = tq // WINDOW
    kern = functools.partial(_attn_kernel, tq=tq)
    return pl.pallas_call(
        kern,
        grid=(bsz, s // tq),
        in_specs=[
            pl.BlockSpec(memory_space=pltpu.SMEM),
            pl.BlockSpec((None, tq, ATT_Q_W), lambda b, i: (b, i, COLBLK_Q)),
            pl.BlockSpec((None, WINDOW, ATT_KV_W), lambda b, i: (b, jnp.maximum(i * sub - 1, 0), COLBLK_K)),
            pl.BlockSpec((None, tq, ATT_KV_W), lambda b, i: (b, i, COLBLK_K)),
            pl.BlockSpec((None, WINDOW, ATT_KV_W), lambda b, i: (b, jnp.maximum(i * sub - 1, 0), COLBLK_V)),
            pl.BlockSpec((None, tq, ATT_KV_W), lambda b, i: (b, i, COLBLK_V)),
        ],
        out_specs=pl.BlockSpec((None, tq, ATT_Q_W), lambda b, i: (b, i, 0)),
        out_shape=jax.ShapeDtypeStruct((bsz, s, ATT_Q_W), BF16),
        compiler_params=_params("parallel", "parallel"),
        name="swa_attention",
    )(sinks, proj, proj, proj, proj, proj)


def _gdn_kernel(dq_ref, dk_ref, dv_ref, z_ref, gate_ref, gatet_ref, convw_ref, acol_ref, dtcol_ref,
                arow_ref, dtrow_ref, nw_ref, o_ref, xpad, state, *, ts):
    t = pl.program_id(1)
    nh, hd = DN_HEADS, DN_HEAD_DIM
    halo = 8

    @pl.when(t == 0)
    def _():
        state[...] = jnp.zeros_like(state)
        for a in range(3):
            xpad[a, 0:halo, :] = jnp.zeros((halo, DN_W), F32)

    @pl.when(t > 0)
    def _():
        for a in range(3):
            xpad[a, 0:halo, :] = xpad[a, ts:ts + halo, :]

    for a, ref in enumerate((dq_ref, dk_ref, dv_ref)):
        xpad[a, halo:halo + ts, :] = ref[...].astype(F32)

    ri = lax.broadcasted_iota(jnp.int32, (CHUNK, CHUNK), 0)
    ci = lax.broadcasted_iota(jnp.int32, (CHUNK, CHUNK), 1)
    causal = ri >= ci
    strict = ri > ci
    tri_lo = causal.astype(F32)
    tri_up = (ri <= ci).astype(F32)
    eye = (ri == ci).astype(F32)
    neg_inf = jnp.float32(-jnp.inf)

    def chunk_body(c, carry):
        r0 = pl.multiple_of(c * CHUNK, CHUNK)
        gcols = gate_ref[pl.ds(r0, CHUNK), :]
        beta_cols = jax.nn.sigmoid(gcols)
        g_cols = -jnp.exp(acol_ref[...]) * jax.nn.softplus(gcols + dtcol_ref[...])
        gc_cols = _dot_f32(tri_lo, g_cols)
        grows = gatet_ref[c]
        g_rows = -jnp.exp(arow_ref[...]) * jax.nn.softplus(grows + dtrow_ref[...])
        gc_rows = _dot_f32(g_rows, tri_up)

        for h in range(nh):
            hs = slice(h * hd, (h + 1) * hd)
            xs = []
            for a in range(3):
                win = xpad[a, pl.ds(r0, CHUNK + halo), hs]
                acc = win[halo:halo + CHUNK] * convw_ref[CONV_K - 1:CONV_K, a * DN_W + h * hd:a * DN_W + (h + 1) * hd]
                for jj in range(CONV_K - 1):
                    sh = CONV_K - 1 - jj
                    acc = acc + win[halo - sh:halo - sh + CHUNK] * convw_ref[jj:jj + 1, a * DN_W + h * hd:a * DN_W + (h + 1) * hd]
                xs.append(_silu(acc))
            q, k, v = xs
            q = q * lax.rsqrt(jnp.sum(q * q, axis=-1, keepdims=True) + RMS_EPS) * (hd ** -0.5)
            k = k * lax.rsqrt(jnp.sum(k * k, axis=-1, keepdims=True) + RMS_EPS)

            beta = jnp.broadcast_to(beta_cols[:, h:h + 1], (CHUNK, hd))
            gcb = jnp.broadcast_to(gc_cols[:, nh + h:nh + h + 1], (CHUNK, hd))
            gc_row = gc_rows[nh + h:nh + h + 1, :]
            gc_last = gc_row[:, CHUNK - 1:CHUNK]
            decay = jnp.exp(jnp.where(causal, gcb[:, :CHUNK] - gc_row, neg_inf))
            egc = jnp.exp(gcb)
            kb = k * beta
            vb = v * beta
            kkqk = _dot_nt(jnp.concatenate([kb, q], axis=0), k)
            l_mat = jnp.where(strict, kkqk[:CHUNK] * decay, 0.0)
            intra = kkqk[CHUNK:] * decay
            m = -l_mat
            xinv = eye + m
            m = _dot(m, m)
            npow = 2
            while npow < CHUNK:
                if npow * 2 < CHUNK:
                    both = _dot(jnp.concatenate([xinv, m], axis=0), m)
                    xinv = xinv + both[:CHUNK]
                    m = both[CHUNK:]
                else:
                    xinv = xinv + _dot(xinv, m)
                npow *= 2
            sol = _dot(xinv, jnp.concatenate([vb, kb * egc], axis=1))
            u, w = sol[:, :hd], sol[:, hd:]
            st = state[h]
            ws_qs = _dot(jnp.concatenate([w, q * egc], axis=0), st)
            v_new = u - ws_qs[:CHUNK]
            o = ws_qs[CHUNK:] + _dot(intra, v_new)
            k_dec = k * jnp.exp(gc_last - gcb)
            state[h] = st * jnp.exp(gc_last) + _dot_tn(k_dec, v_new)
            o = o * lax.rsqrt(jnp.mean(o * o, axis=-1, keepdims=True) + RMS_EPS) * nw_ref[...]
            o = o * _silu(z_ref[pl.ds(r0, CHUNK), hs].astype(F32))
            o_ref[pl.ds(r0, CHUNK), hs] = o.astype(BF16)
        return carry

    lax.fori_loop(0, ts // CHUNK, chunk_body, 0)


def _gated_deltanet(proj, gate, gate_t, conv_w, a_log, dt_bias, norm_w):
    bsz, s, _ = proj.shape
    ts = min(512, s)
    nh = DN_HEADS
    zpad = jnp.zeros((nh,), F32)
    acol = jnp.concatenate([zpad, a_log, jnp.zeros((LANES - 2 * nh,), F32)]).reshape(1, LANES)
    dtcol = jnp.concatenate([zpad, dt_bias, jnp.zeros((LANES - 2 * nh,), F32)]).reshape(1, LANES)
    arow = jnp.broadcast_to(jnp.concatenate([zpad, a_log]).reshape(GATE_LANES, 1), (GATE_LANES, CHUNK))
    dtrow = jnp.broadcast_to(jnp.concatenate([zpad, dt_bias]).reshape(GATE_LANES, 1), (GATE_LANES, CHUNK))
    kern = functools.partial(_gdn_kernel, ts=ts)
    const = lambda b, t: (0, 0)
    return pl.pallas_call(
        kern,
        grid=(bsz, s // ts),
        in_specs=[
            pl.BlockSpec((None, ts, DN_W), lambda b, t: (b, t, COLBLK_DQ)),
            pl.BlockSpec((None, ts, DN_W), lambda b, t: (b, t, COLBLK_DK)),
            pl.BlockSpec((None, ts, DN_W), lambda b, t: (b, t, COLBLK_DV)),
            pl.BlockSpec((None, ts, DN_W), lambda b, t: (b, t, COLBLK_Z)),
            pl.BlockSpec((None, ts, LANES), lambda b, t: (b, t, 0)),
            pl.BlockSpec((None, ts // CHUNK, GATE_LANES, CHUNK), lambda b, t: (b, t, 0, 0)),
            pl.BlockSpec((CONV_K, 3 * DN_W), const),
            pl.BlockSpec((1, LANES), const),
            pl.BlockSpec((1, LANES), const),
            pl.BlockSpec((GATE_LANES, CHUNK), const),
            pl.BlockSpec((GATE_LANES, CHUNK), const),
            pl.BlockSpec((1, DN_HEAD_DIM), const),
        ],
        out_specs=pl.BlockSpec((None, ts, DN_W), lambda b, t: (b, t, 0)),
        out_shape=jax.ShapeDtypeStruct((bsz, s, DN_W), BF16),
        scratch_shapes=[
            pltpu.VMEM((3, ts + 8, DN_W), F32),
            pltpu.VMEM((DN_HEADS, DN_HEAD_DIM, DN_HEAD_DIM), F32),
        ],
        compiler_params=_params("parallel", "arbitrary"),
        name="gated_deltanet",
    )(proj, proj, proj, proj, gate, gate_t, conv_w, acol, dtcol, arow, dtrow, norm_w.reshape(1, DN_HEAD_DIM))


def _mix_kernel(att_ref, gdn_ref, ga_ref, gb_ref, x_ref, mod_ref, woa_ref, wob_ref, wout_ref, g_ref, b_ref, o_ref):
    y_a = jnp.dot(att_ref[...], woa_ref[...], preferred_element_type=F32)
    y_b = jnp.dot(gdn_ref[...], wob_ref[...], preferred_element_type=F32)
    merged = jax.nn.sigmoid(ga_ref[...].astype(F32)) * y_a + jax.nn.sigmoid(gb_ref[...].astype(F32)) * y_b
    mixed = jnp.dot(merged.astype(BF16), wout_ref[...], preferred_element_type=F32)
    o_ref[...] = _layer_norm(ALPHA * x_ref[...] + (1.0 + mod_ref[2:3, :]) * mixed, g_ref[...], b_ref[...])


def _mix(att, gdn, proj, x, mod_l, w_oa, w_ob, w_out, ln_g, ln_b):
    bsz, s, d = x.shape
    tm = min(512, s)
    row = lambda b, i: (b, i, 0)
    const = lambda b, i: (0, 0)
    return pl.pallas_call(
        _mix_kernel,
        grid=(bsz, s // tm),
        in_specs=[
            pl.BlockSpec((None, tm, d), row),
            pl.BlockSpec((None, tm, d), row),
            pl.BlockSpec((None, tm, d), lambda b, i: (b, i, COLBLK_GA)),
            pl.BlockSpec((None, tm, d), lambda b, i: (b, i, COLBLK_GB)),
            pl.BlockSpec((None, tm, d), row),
            pl.BlockSpec((None, 6, d), lambda b, i: (b, 0, 0)),
            pl.BlockSpec((d, d), const),
            pl.BlockSpec((d, d), const),
            pl.BlockSpec((d, d), const),
            pl.BlockSpec((1, d), const),
            pl.BlockSpec((1, d), const),
        ],
        out_specs=pl.BlockSpec((None, tm, d), row),
        out_shape=jax.ShapeDtypeStruct((bsz, s, d), F32),
        compiler_params=_params("parallel", "parallel"),
        name="mix_out_ln",
    )(att, gdn, proj, proj, x, mod_l, w_oa, w_ob, w_out, ln_g.reshape(1, d), ln_b.reshape(1, d))


def _ffn_kernel(x_ref, mod_ref, w1_ref, b1_ref, w2_ref, b2_ref, g_ref, b_ref, o_ref, u_scr, acc):
    f = pl.program_id(2)

    @pl.when(f == 0)
    def _():
        u_scr[...] = (x_ref[...] * (1.0 + mod_ref[4:5, :]) + mod_ref[3:4, :]).astype(BF16)
        acc[...] = jnp.zeros_like(acc)

    h = jnp.dot(u_scr[...], w1_ref[...], preferred_element_type=F32) + b1_ref[...]
    h = jnp.square(jnp.maximum(h, 0.0))
    acc[...] += jnp.dot(h.astype(BF16), w2_ref[...], preferred_element_type=F32)

    @pl.when(f == pl.num_programs(2) - 1)
    def _():
        y = acc[...] + b2_ref[...]
        o_ref[...] = _layer_norm(ALPHA * x_ref[...] + (1.0 + mod_ref[5:6, :]) * y, g_ref[...], b_ref[...])


def _ffn(x, mod_l, w1, b1, w2, b2, ln_g, ln_b):
    bsz, s, d = x.shape
    dff = w1.shape[1]
    tm = min(1024, s)
    tf = 1024
    row = lambda b, i, f: (b, i, 0)
    const = lambda b, i, f: (0, 0)
    return pl.pallas_call(
        _ffn_kernel,
        grid=(bsz, s // tm, dff // tf),
        in_specs=[
            pl.BlockSpec((None, tm, d), row),
            pl.BlockSpec((None, 6, d), lambda b, i, f: (b, 0, 0)),
            pl.BlockSpec((d, tf), lambda b, i, f: (0, f)),
            pl.BlockSpec((1, tf), lambda b, i, f: (0, f)),
            pl.BlockSpec((tf, d), lambda b, i, f: (f, 0)),
            pl.BlockSpec((1, d), const),
            pl.BlockSpec((1, d), const),
            pl.BlockSpec((1, d), const),
        ],
        out_specs=pl.BlockSpec((None, tm, d), row),
        out_shape=jax.ShapeDtypeStruct((bsz, s, d), F32),
        scratch_shapes=[pltpu.VMEM((tm, d), BF16), pltpu.VMEM((tm, d), F32)],
        compiler_params=_params("parallel", "parallel", "arbitrary"),
        name="ffn_ln",
    )(x, mod_l, w1, b1.reshape(1, dff), w2, b2.reshape(1, d), ln_g.reshape(1, d), ln_b.reshape(1, d))


def kernel(x, c, w_ada, b_ada, w_in, conv_w, a_log, dt_bias, sinks, dn_norm_w, w_oa, w_ob, w_out,
           ln1_g, ln1_b, w_ff1, b_ff1, w_ff2, b_ff2, ln2_g, ln2_b):
    depth = w_ada.shape[0]
    bsz, s, d = x.shape
    mod = _ada_mod(c, w_ada, b_ada).reshape(depth, bsz, 6, d)
    gate_lo = ATT_Q_W + 2 * ATT_KV_W + 4 * DN_W
    gate_hi = gate_lo + GATE_LANES
    for l in range(depth):
        wl = w_in[l]
        w_main = jnp.concatenate(
            [wl[:, :ATT_Q_W], wl[:, ATT_Q_W + 2 * ATT_KV_W:gate_lo], wl[:, gate_hi:], wl[:, ATT_Q_W:ATT_Q_W + 2 * ATT_KV_W]],
            axis=1).astype(BF16)
        w_gate_t = wl[:, gate_lo:gate_hi].T.astype(BF16)
        w_gate = jnp.pad(wl[:, gate_lo:gate_hi], ((0, 0), (0, LANES - GATE_LANES))).astype(BF16)
        proj, gate, gate_t = _in_proj(x, mod[l], w_main, w_gate, w_gate_t)
        att = _attention(proj, sinks[l])
        gdn = _gated_deltanet(proj, gate, gate_t, conv_w[l], a_log[l], dt_bias[l], dn_norm_w[l])
        x = _mix(att, gdn, proj, x, mod[l], w_oa[l].astype(BF16), w_ob[l].astype(BF16), w_out[l].astype(BF16),
                 ln1_g[l], ln1_b[l])
        x = _ffn(x, mod[l], w_ff1[l].astype(BF16), b_ff1[l], w_ff2[l].astype(BF16), b_ff2[l], ln2_g[l], ln2_b[l])
    return x
```

```python
import functools

import jax
import jax.numpy as jnp
from jax import lax
from jax.experimental import pallas as pl
from jax.experimental.pallas import tpu as pltpu

F32 = jnp.float32
BF16 = jnp.bfloat16

D_MODEL = 1024
DEPTH = 4
ATT_HEADS = 16
ATT_KV_HEADS = 4
ATT_GROUP = ATT_HEADS // ATT_KV_HEADS
ATT_HEAD_DIM = 64
WINDOW = 128
DN_HEADS = 8
DN_HEAD_DIM = 128
CONV_K = 4
CHUNK = 64
D_FF = 4 * D_MODEL
ATT_Q_W = ATT_HEADS * ATT_HEAD_DIM
ATT_KV_W = ATT_KV_HEADS * ATT_HEAD_DIM
DN_W = DN_HEADS * DN_HEAD_DIM
ALPHA = (2 * DEPTH) ** 0.25
LN_EPS = 1e-5
RMS_EPS = 1e-6

LANES = 128
GATE_LANES = 2 * DN_HEADS
N_MAIN = ATT_Q_W + 4 * DN_W + 2 * D_MODEL + 2 * ATT_KV_W
COLBLK_Q, COLBLK_DQ, COLBLK_DK, COLBLK_DV, COLBLK_Z, COLBLK_GA, COLBLK_GB = range(7)
COLBLK_K = (ATT_Q_W + 4 * DN_W + 2 * D_MODEL) // ATT_KV_W
COLBLK_V = COLBLK_K + 1
VMEM_LIMIT = 48 * 1024 * 1024


def _params(*sem):
    return pltpu.CompilerParams(dimension_semantics=sem, vmem_limit_bytes=VMEM_LIMIT)


def _dot(a, b):
    return jnp.dot(a.astype(BF16), b.astype(BF16), preferred_element_type=F32)


def _dot_nt(a, b):
    return lax.dot_general(a.astype(BF16), b.astype(BF16), (((1,), (1,)), ((), ())), preferred_element_type=F32)


def _dot_tn(a, b):
    return lax.dot_general(a.astype(BF16), b.astype(BF16), (((0,), (0,)), ((), ())), preferred_element_type=F32)


def _dot_f32(a, b):
    return jnp.dot(a, b, precision=lax.Precision.HIGHEST, preferred_element_type=F32)


def _silu(t):
    return t * jax.nn.sigmoid(t)


def _layer_norm(t, g, b):
    mu = jnp.mean(t, axis=-1, keepdims=True)
    tc = t - mu
    var = jnp.mean(tc * tc, axis=-1, keepdims=True)
    return tc * lax.rsqrt(var + LN_EPS) * g + b


def _ada_kernel(c_ref, w_ref, b_ref, o_ref):
    ca = _silu(c_ref[...])
    o_ref[...] = _dot(ca, w_ref[...]) + b_ref[...]


def _ada_mod(c, w_ada, b_ada):
    depth, d, n = w_ada.shape
    bsz = c.shape[0]
    tn = 1536
    return pl.pallas_call(
        _ada_kernel,
        grid=(depth, n // tn),
        in_specs=[
            pl.BlockSpec((bsz, d), lambda l, j: (0, 0)),
            pl.BlockSpec((None, d, tn), lambda l, j: (l, 0, j)),
            pl.BlockSpec((None, 1, tn), lambda l, j: (l, 0, j)),
        ],
        out_specs=pl.BlockSpec((None, bsz, tn), lambda l, j: (l, 0, j)),
        out_shape=jax.ShapeDtypeStruct((depth, bsz, n), F32),
        compiler_params=_params("parallel", "parallel"),
        name="ada_mod",
    )(c, w_ada, b_ada.reshape(depth, 1, n))


def _inproj_kernel(x_ref, mod_ref, w_ref, wg_ref, wgt_ref, proj_ref, gate_ref, gatet_ref, u_scr):
    j = pl.program_id(2)

    @pl.when(j == 0)
    def _():
        u = (x_ref[...] * (1.0 + mod_ref[1:2, :]) + mod_ref[0:1, :]).astype(BF16)
        u_scr[...] = u
        gate_ref[...] = jnp.dot(u, wg_ref[...], preferred_element_type=F32)
        for ci in range(u.shape[0] // CHUNK):
            uc = u[ci * CHUNK:(ci + 1) * CHUNK]
            gatet_ref[ci] = lax.dot_general(wgt_ref[...], uc, (((1,), (1,)), ((), ())), preferred_element_type=F32)

    proj_ref[...] = jnp.dot(u_scr[...], w_ref[...], preferred_element_type=F32).astype(BF16)


def _in_proj(x, mod_l, w_main, w_gate, w_gate_t):
    bsz, s, d = x.shape
    tm = min(1024, s)
    tn = 1536
    return pl.pallas_call(
        _inproj_kernel,
        grid=(bsz, s // tm, N_MAIN // tn),
        in_specs=[
            pl.BlockSpec((None, tm, d), lambda b, i, j: (b, i, 0)),
            pl.BlockSpec((None, 6, d), lambda b, i, j: (b, 0, 0)),
            pl.BlockSpec((d, tn), lambda b, i, j: (0, j)),
            pl.BlockSpec((d, LANES), lambda b, i, j: (0, 0)),
            pl.BlockSpec((GATE_LANES, d), lambda b, i, j: (0, 0)),
        ],
        out_specs=[
            pl.BlockSpec((None, tm, tn), lambda b, i, j: (b, i, j)),
            pl.BlockSpec((None, tm, LANES), lambda b, i, j: (b, i, 0)),
            pl.BlockSpec((None, tm // CHUNK, GATE_LANES, CHUNK), lambda b, i, j: (b, i, 0, 0)),
        ],
        out_shape=[
            jax.ShapeDtypeStruct((bsz, s, N_MAIN), BF16),
            jax.ShapeDtypeStruct((bsz, s, LANES), F32),
            jax.ShapeDtypeStruct((bsz, s // CHUNK, GATE_LANES, CHUNK), F32),
        ],
        scratch_shapes=[pltpu.VMEM((tm, d), BF16)],
        compiler_params=_params("parallel", "parallel", "arbitrary"),
        name="in_proj",
    )(x, mod_l, w_main, w_gate, w_gate_t)


def _attn_kernel(sink_ref, q_ref, kp_ref, kc_ref, vp_ref, vc_ref, o_ref, *, tq):
    i = pl.program_id(1)
    nsub = tq // WINDOW
    hd, grp, kvh = ATT_HEAD_DIM, ATT_GROUP, ATT_KV_HEADS
    kfull = jnp.concatenate([kp_ref[...], kc_ref[...]], axis=0).astype(F32)
    vfull = jnp.concatenate([vp_ref[...], vc_ref[...]], axis=0).astype(F32)
    lane = lax.broadcasted_iota(jnp.int32, kfull.shape, 1)
    k_rolled = [kfull] + [pltpu.roll(kfull, r * hd, axis=1) for r in range(1, kvh)]
    v_rolled = [vfull] + [pltpu.roll(vfull, r * hd, axis=1) for r in range(1, kvh)]

    def placed(rolled, kv, g):
        src = rolled[(g - kv) % kvh]
        return jnp.where((lane >= g * hd) & (lane < (g + 1) * hd), src, 0.0).astype(BF16)

    qi = lax.broadcasted_iota(jnp.int32, (WINDOW, 2 * WINDOW), 0)
    si = lax.broadcasted_iota(jnp.int32, (WINDOW, 2 * WINDOW), 1)
    diff = qi + WINDOW - si
    band = (diff >= 0) & (diff < WINDOW)
    band0 = band & ((si >= WINDOW) | (i > 0))
    neg_inf = jnp.float32(-jnp.inf)

    for kv in range(kvh):
        kp = [placed(k_rolled, kv, g) for g in range(grp)]
        vp = [placed(v_rolled, kv, g) for g in range(grp)]
        for s in range(nsub):
            r0, r1 = s * WINDOW, (s + 2) * WINDOW
            kbd = jnp.concatenate([kp[g][r0:r1] for g in range(grp)], axis=0)
            vbd = jnp.concatenate([vp[g][r0:r1] for g in range(grp)], axis=0)
            qs = q_ref[s * WINDOW:(s + 1) * WINDOW, kv * grp * hd:(kv + 1) * grp * hd]
            sc = lax.dot_general(qs, kbd, (((1,), (1,)), ((), ())), preferred_element_type=F32)
            sc = sc * (hd ** -0.5)
            valid = band0 if s == 0 else band
            probs = []
            for g in range(grp):
                sg = jnp.where(valid, sc[:, g * 2 * WINDOW:(g + 1) * 2 * WINDOW], neg_inf)
                sink = sink_ref[kv * grp + g]
                m = jnp.maximum(jnp.max(sg, axis=-1, keepdims=True), sink)
                p = jnp.exp(sg - m)
                denom = jnp.sum(p, axis=-1, keepdims=True) + jnp.exp(sink - m)
                probs.append((p / denom).astype(BF16))
            pcat = jnp.concatenate(probs, axis=1)
            o = jnp.dot(pcat, vbd, preferred_element_type=F32)
            o_ref[s * WINDOW:(s + 1) * WINDOW, kv * grp * hd:(kv + 1) * grp * hd] = o.astype(BF16)


def _attention(proj, sinks):
    bsz, s, _ = proj.shape
    tq = min(512, s)
    sub = tq // WINDOW
    kern = functools.partial(_attn_kernel, tq=tq)
    return pl.pallas_call(
        kern,
        grid=(bsz, s // tq),
        in_specs=[
            pl.BlockSpec(memory_space=pltpu.SMEM),
            pl.BlockSpec((None, tq, ATT_Q_W), lambda b, i: (b, i, COLBLK_Q)),
            pl.BlockSpec((None, WINDOW, ATT_KV_W), lambda b, i: (b, jnp.maximum(i * sub - 1, 0), COLBLK_K)),
            pl.BlockSpec((None, tq, ATT_KV_W), lambda b, i: (b, i, COLBLK_K)),
            pl.BlockSpec((None, WINDOW, ATT_KV_W), lambda b, i: (b, jnp.maximum(i * sub - 1, 0), COLBLK_V)),
            pl.BlockSpec((None, tq, ATT_KV_W), lambda b, i: (b, i, COLBLK_V)),
        ],
        out_specs=pl.BlockSpec((None, tq, ATT_Q_W), lambda b, i: (b, i, 0)),
        out_shape=jax.ShapeDtypeStruct((bsz, s, ATT_Q_W), BF16),
        compiler_params=_params("parallel", "parallel"),
        name="swa_attention",
    )(sinks, proj, proj, proj, proj, proj)


def _gdn_kernel(dq_ref, dk_ref, dv_ref, z_ref, gate_ref, gatet_ref, convw_ref, acol_ref, dtcol_ref,
                arow_ref, dtrow_ref, nw_ref, o_ref, xpad, state, *, ts):
    t = pl.program_id(1)
    nh, hd = DN_HEADS, DN_HEAD_DIM
    halo = 8

    @pl.when(t == 0)
    def _():
        state[...] = jnp.zeros_like(state)
        for a in range(3):
            xpad[a, 0:halo, :] = jnp.zeros((halo, DN_W), F32)

    @pl.when(t > 0)
    def _():
        for a in range(3):
            xpad[a, 0:halo, :] = xpad[a, ts:ts + halo, :]

    for a, ref in enumerate((dq_ref, dk_ref, dv_ref)):
        xpad[a, halo:halo + ts, :] = ref[...].astype(F32)

    ri = lax.broadcasted_iota(jnp.int32, (CHUNK, CHUNK), 0)
    ci = lax.broadcasted_iota(jnp.int32, (CHUNK, CHUNK), 1)
    causal = ri >= ci
    strict = ri > ci
    tri_lo = causal.astype(F32)
    tri_up = (ri <= ci).astype(F32)
    eye = (ri == ci).astype(F32)
    neg_inf = jnp.float32(-jnp.inf)

    def chunk_body(c, carry):
        r0 = pl.multiple_of(c * CHUNK, CHUNK)
        gcols = gate_ref[pl.ds(r0, CHUNK), :]
        beta_cols = jax.nn.sigmoid(gcols)
        g_cols = -jnp.exp(acol_ref[...]) * jax.nn.softplus(gcols + dtcol_ref[...])
        gc_cols = _dot_f32(tri_lo, g_cols)
        grows = gatet_ref[c]
        g_rows = -jnp.exp(arow_ref[...]) * jax.nn.softplus(grows + dtrow_ref[...])
        gc_rows = _dot_f32(g_rows, tri_up)

        heads = range(nh)
        hsl = [slice(h * hd, (h + 1) * hd) for h in heads]
        qs, ks, vs = [], [], []
        for h in heads:
            xs = []
            for a in range(3):
                cs = slice(a * DN_W + h * hd, a * DN_W + (h + 1) * hd)
                win = xpad[a, pl.ds(r0, CHUNK + halo), hsl[h]]
                acc = win[halo:halo + CHUNK] * convw_ref[CONV_K - 1:CONV_K, cs]
                for jj in range(CONV_K - 1):
                    sh = CONV_K - 1 - jj
                    acc = acc + win[halo - sh:halo - sh + CHUNK] * convw_ref[jj:jj + 1, cs]
                xs.append(_silu(acc))
            q, k, v = xs
            qs.append(q * lax.rsqrt(jnp.sum(q * q, axis=-1, keepdims=True) + RMS_EPS) * (hd ** -0.5))
            ks.append(k * lax.rsqrt(jnp.sum(k * k, axis=-1, keepdims=True) + RMS_EPS))
            vs.append(v)

        gcbs = [jnp.broadcast_to(gc_cols[:, nh + h:nh + h + 1], (CHUNK, hd)) for h in heads]
        gc_lasts = [gc_rows[nh + h:nh + h + 1, CHUNK - 1:CHUNK] for h in heads]
        decays = [jnp.exp(jnp.where(causal, gcbs[h][:, :CHUNK] - gc_rows[nh + h:nh + h + 1, :], neg_inf)) for h in heads]
        egcs = [jnp.exp(gcbs[h]) for h in heads]
        betas = [jnp.broadcast_to(beta_cols[:, h:h + 1], (CHUNK, hd)) for h in heads]
        kbs = [ks[h] * betas[h] for h in heads]
        vbs = [vs[h] * betas[h] for h in heads]

        kkqk = [_dot_nt(jnp.concatenate([kbs[h], qs[h]], axis=0), ks[h]) for h in heads]
        intras = [kkqk[h][CHUNK:] * decays[h] for h in heads]
        ms = [-jnp.where(strict, kkqk[h][:CHUNK] * decays[h], 0.0) for h in heads]
        xinvs = [eye + ms[h] for h in heads]
        ms = [_dot(ms[h], ms[h]) for h in heads]
        npow = 2
        while npow < CHUNK:
            if npow * 2 < CHUNK:
                both = [_dot(jnp.concatenate([xinvs[h], ms[h]], axis=0), ms[h]) for h in heads]
                xinvs = [xinvs[h] + both[h][:CHUNK] for h in heads]
                ms = [both[h][CHUNK:] for h in heads]
            else:
                xinvs = [xinvs[h] + _dot(xinvs[h], ms[h]) for h in heads]
            npow *= 2
        sols = [_dot(xinvs[h], jnp.concatenate([vbs[h], kbs[h] * egcs[h]], axis=1)) for h in heads]
        sts = [state[h] for h in heads]
        ws_qs = [_dot(jnp.concatenate([sols[h][:, hd:], qs[h] * egcs[h]], axis=0), sts[h]) for h in heads]
        v_news = [sols[h][:, :hd] - ws_qs[h][:CHUNK] for h in heads]
        k_decs = [ks[h] * jnp.exp(gc_lasts[h] - gcbs[h]) for h in heads]
        upds = [_dot_tn(k_decs[h], v_news[h]) for h in heads]
        outs = [ws_qs[h][CHUNK:] + _dot(intras[h], v_news[h]) for h in heads]
        for h in heads:
            state[h] = sts[h] * jnp.exp(gc_lasts[h]) + upds[h]
        for h in heads:
            o = outs[h]
            o = o * lax.rsqrt(jnp.mean(o * o, axis=-1, keepdims=True) + RMS_EPS) * nw_ref[...]
            o = o * _silu(z_ref[pl.ds(r0, CHUNK), hsl[h]].astype(F32))
            o_ref[pl.ds(r0, CHUNK), hsl[h]] = o.astype(BF16)
        return carry

    lax.fori_loop(0, ts // CHUNK, chunk_body, 0)


def _gated_deltanet(proj, gate, gate_t, conv_w, a_log, dt_bias, norm_w):
    bsz, s, _ = proj.shape
    ts = min(512, s)
    nh = DN_HEADS
    zpad = jnp.zeros((nh,), F32)
    acol = jnp.concatenate([zpad, a_log, jnp.zeros((LANES - 2 * nh,), F32)]).reshape(1, LANES)
    dtcol = jnp.concatenate([zpad, dt_bias, jnp.zeros((LANES - 2 * nh,), F32)]).reshape(1, LANES)
    arow = jnp.broadcast_to(jnp.concatenate([zpad, a_log]).reshape(GATE_LANES, 1), (GATE_LANES, CHUNK))
    dtrow = jnp.broadcast_to(jnp.concatenate([zpad, dt_bias]).reshape(GATE_LANES, 1), (GATE_LANES, CHUNK))
    kern = functools.partial(_gdn_kernel, ts=ts)
    const = lambda b, t: (0, 0)
    return pl.pallas_call(
        kern,
        grid=(bsz, s // ts),
        in_specs=[
            pl.BlockSpec((None, ts, DN_W), lambda b, t: (b, t, COLBLK_DQ)),
            pl.BlockSpec((None, ts, DN_W), lambda b, t: (b, t, COLBLK_DK)),
            pl.BlockSpec((None, ts, DN_W), lambda b, t: (b, t, COLBLK_DV)),
            pl.BlockSpec((None, ts, DN_W), lambda b, t: (b, t, COLBLK_Z)),
            pl.BlockSpec((None, ts, LANES), lambda b, t: (b, t, 0)),
            pl.BlockSpec((None, ts // CHUNK, GATE_LANES, CHUNK), lambda b, t: (b, t, 0, 0)),
            pl.BlockSpec((CONV_K, 3 * DN_W), const),
            pl.BlockSpec((1, LANES), const),
            pl.BlockSpec((1, LANES), const),
            pl.BlockSpec((GATE_LANES, CHUNK), const),
            pl.BlockSpec((GATE_LANES, CHUNK), const),
            pl.BlockSpec((1, DN_HEAD_DIM), const),
        ],
        out_specs=pl.BlockSpec((None, ts, DN_W), lambda b, t: (b, t, 0)),
        out_shape=jax.ShapeDtypeStruct((bsz, s, DN_W), BF16),
        scratch_shapes=[
            pltpu.VMEM((3, ts + 8, DN_W), F32),
            pltpu.VMEM((DN_HEADS, DN_HEAD_DIM, DN_HEAD_DIM), F32),
        ],
        compiler_params=_params("parallel", "arbitrary"),
        name="gated_deltanet",
    )(proj, proj, proj, proj, gate, gate_t, conv_w, acol, dtcol, arow, dtrow, norm_w.reshape(1, DN_HEAD_DIM))


def _mix_kernel(att_ref, gdn_ref, ga_ref, gb_ref, x_ref, mod_ref, woa_ref, wob_ref, wout_ref, g_ref, b_ref, o_ref):
    y_a = jnp.dot(att_ref[...], woa_ref[...], preferred_element_type=F32)
    y_b = jnp.dot(gdn_ref[...], wob_ref[...], preferred_element_type=F32)
    merged = jax.nn.sigmoid(ga_ref[...].astype(F32)) * y_a + jax.nn.sigmoid(gb_ref[...].astype(F32)) * y_b
    mixed = jnp.dot(merged.astype(BF16), wout_ref[...], preferred_element_type=F32)
    o_ref[...] = _layer_norm(ALPHA * x_ref[...] + (1.0 + mod_ref[2:3, :]) * mixed, g_ref[...], b_ref[...])


def _mix(att, gdn, proj, x, mod_l, w_oa, w_ob, w_out, ln_g, ln_b):
    bsz, s, d = x.shape
    tm = min(512, s)
    row = lambda b, i: (b, i, 0)
    const = lambda b, i: (0, 0)
    return pl.pallas_call(
        _mix_kernel,
        grid=(bsz, s // tm),
        in_specs=[
            pl.BlockSpec((None, tm, d), row),
            pl.BlockSpec((None, tm, d), row),
            pl.BlockSpec((None, tm, d), lambda b, i: (b, i, COLBLK_GA)),
            pl.BlockSpec((None, tm, d), lambda b, i: (b, i, COLBLK_GB)),
            pl.BlockSpec((None, tm, d), row),
            pl.BlockSpec((None, 6, d), lambda b, i: (b, 0, 0)),
            pl.BlockSpec((d, d), const),
            pl.BlockSpec((d, d), const),
            pl.BlockSpec((d, d), const),
            pl.BlockSpec((1, d), const),
            pl.BlockSpec((1, d), const),
        ],
        out_specs=pl.BlockSpec((None, tm, d), row),
        out_shape=jax.ShapeDtypeStruct((bsz, s, d), F32),
        compiler_params=_params("parallel", "parallel"),
        name="mix_out_ln",
    )(att, gdn, proj, proj, x, mod_l, w_oa, w_ob, w_out, ln_g.reshape(1, d), ln_b.reshape(1, d))


def _ffn_kernel(x_ref, mod_ref, w1_ref, b1_ref, w2_ref, b2_ref, g_ref, b_ref, o_ref, u_scr, acc):
    f = pl.program_id(2)

    @pl.when(f == 0)
    def _():
        u_scr[...] = (x_ref[...] * (1.0 + mod_ref[4:5, :]) + mod_ref[3:4, :]).astype(BF16)
        acc[...] = jnp.zeros_like(acc)

    h = jnp.dot(u_scr[...], w1_ref[...], preferred_element_type=F32) + b1_ref[...]
    h = jnp.square(jnp.maximum(h, 0.0))
    acc[...] += jnp.dot(h.astype(BF16), w2_ref[...], preferred_element_type=F32)

    @pl.when(f == pl.num_programs(2) - 1)
    def _():
        y = acc[...] + b2_ref[...]
        o_ref[...] = _layer_norm(ALPHA * x_ref[...] + (1.0 + mod_ref[5:6, :]) * y, g_ref[...], b_ref[...])


def _ffn(x, mod_l, w1, b1, w2, b2, ln_g, ln_b):
    bsz, s, d = x.shape
    dff = w1.shape[1]
    tm = min(1024, s)
    tf = 1024
    row = lambda b, i, f: (b, i, 0)
    const = lambda b, i, f: (0, 0)
    return pl.pallas_call(
        _ffn_kernel,
        grid=(bsz, s // tm, dff // tf),
        in_specs=[
            pl.BlockSpec((None, tm, d), row),
            pl.BlockSpec((None, 6, d), lambda b, i, f: (b, 0, 0)),
            pl.BlockSpec((d, tf), lambda b, i, f: (0, f)),
            pl.BlockSpec((1, tf), lambda b, i, f: (0, f)),
            pl.BlockSpec((tf, d), lambda b, i, f: (f, 0)),
            pl.BlockSpec((1, d), const),
            pl.BlockSpec((1, d), const),
            pl.BlockSpec((1, d), const),
        ],
        out_specs=pl.BlockSpec((None, tm, d), row),
        out_shape=jax.ShapeDtypeStruct((bsz, s, d), F32),
        scratch_shapes=[pltpu.VMEM((tm, d), BF16), pltpu.VMEM((tm, d), F32)],
        compiler_params=_params("parallel", "parallel", "arbitrary"),
        name="ffn_ln",
    )(x, mod_l, w1, b1.reshape(1, dff), w2, b2.reshape(1, d), ln_g.reshape(1, d), ln_b.reshape(1, d))


def kernel(x, c, w_ada, b_ada, w_in, conv_w, a_log, dt_bias, sinks, dn_norm_w, w_oa, w_ob, w_out,
           ln1_g, ln1_b, w_ff1, b_ff1, w_ff2, b_ff2, ln2_g, ln2_b):
    depth = w_ada.shape[0]
    bsz, s, d = x.shape
    mod = _ada_mod(c, w_ada, b_ada).reshape(depth, bsz, 6, d)
    gate_lo = ATT_Q_W + 2 * ATT_KV_W + 4 * DN_W
    gate_hi = gate_lo + GATE_LANES
    for l in range(depth):
        wl = w_in[l]
        w_main = jnp.concatenate(
            [wl[:, :ATT_Q_W], wl[:, ATT_Q_W + 2 * ATT_KV_W:gate_lo], wl[:, gate_hi:], wl[:, ATT_Q_W:ATT_Q_W + 2 * ATT_KV_W]],
            axis=1).astype(BF16)
        w_gate_t = wl[:, gate_lo:gate_hi].T.astype(BF16)
        w_gate = jnp.pad(wl[:, gate_lo:gate_hi], ((0, 0), (0, LANES - GATE_LANES))).astype(BF16)
        proj, gate, gate_t = _in_proj(x, mod[l], w_main, w_gate, w_gate_t)
        att = _attention(proj, sinks[l])
        gdn = _gated_deltanet(proj, gate, gate_t, conv_w[l], a_log[l], dt_bias[l], dn_norm_w[l])
        x = _mix(att, gdn, proj, x, mod[l], w_oa[l].astype(BF16), w_ob[l].astype(BF16), w_out[l].astype(BF16),
                 ln1_g[l], ln1_b[l])
        x = _ffn(x, mod[l], w_ff1[l].astype(BF16), b_ff1[l], w_ff2[l].astype(BF16), b_ff2[l], ln2_g[l], ln2_b[l])
    return x
```

```python
import functools

import jax
import jax.numpy as jnp
from jax import lax
from jax.experimental import pallas as pl
from jax.experimental.pallas import tpu as pltpu

F32 = jnp.float32
BF16 = jnp.bfloat16

D_MODEL = 1024
DEPTH = 4
ATT_HEADS = 16
ATT_KV_HEADS = 4
ATT_GROUP = ATT_HEADS // ATT_KV_HEADS
ATT_HEAD_DIM = 64
WINDOW = 128
DN_HEADS = 8
DN_HEAD_DIM = 128
CONV_K = 4
CHUNK = 64
D_FF = 4 * D_MODEL
ATT_Q_W = ATT_HEADS * ATT_HEAD_DIM
ATT_KV_W = ATT_KV_HEADS * ATT_HEAD_DIM
DN_W = DN_HEADS * DN_HEAD_DIM
ALPHA = (2 * DEPTH) ** 0.25
LN_EPS = 1e-5
RMS_EPS = 1e-6

LANES = 128
GATE_LANES = 2 * DN_HEADS
N_MAIN = ATT_Q_W + 4 * DN_W + 2 * D_MODEL + 2 * ATT_KV_W
COLBLK_Q, COLBLK_DQ, COLBLK_DK, COLBLK_DV, COLBLK_Z, COLBLK_GA, COLBLK_GB = range(7)
COLBLK_K = (ATT_Q_W + 4 * DN_W + 2 * D_MODEL) // ATT_KV_W
COLBLK_V = COLBLK_K + 1
VMEM_LIMIT = 48 * 1024 * 1024


def _params(*sem):
    return pltpu.CompilerParams(dimension_semantics=sem, vmem_limit_bytes=VMEM_LIMIT)


def _dot(a, b):
    return jnp.dot(a.astype(BF16), b.astype(BF16), preferred_element_type=F32)


def _dot_nt(a, b):
    return lax.dot_general(a.astype(BF16), b.astype(BF16), (((1,), (1,)), ((), ())), preferred_element_type=F32)


def _dot_tn(a, b):
    return lax.dot_general(a.astype(BF16), b.astype(BF16), (((0,), (0,)), ((), ())), preferred_element_type=F32)


def _dot_f32(a, b):
    return jnp.dot(a, b, precision=lax.Precision.HIGHEST, preferred_element_type=F32)


def _silu(t):
    return t * jax.nn.sigmoid(t)


def _layer_norm(t, g, b):
    mu = jnp.mean(t, axis=-1, keepdims=True)
    tc = t - mu
    var = jnp.mean(tc * tc, axis=-1, keepdims=True)
    return tc * lax.rsqrt(var + LN_EPS) * g + b


def _ada_kernel(c_ref, w_ref, b_ref, o_ref):
    ca = _silu(c_ref[...])
    o_ref[...] = _dot(ca, w_ref[...]) + b_ref[...]


def _ada_mod(c, w_ada, b_ada):
    depth, d, n = w_ada.shape
    bsz = c.shape[0]
    tn = 1536
    return pl.pallas_call(
        _ada_kernel,
        grid=(depth, n // tn),
        in_specs=[
            pl.BlockSpec((bsz, d), lambda l, j: (0, 0)),
            pl.BlockSpec((None, d, tn), lambda l, j: (l, 0, j)),
            pl.BlockSpec((None, 1, tn), lambda l, j: (l, 0, j)),
        ],
        out_specs=pl.BlockSpec((None, bsz, tn), lambda l, j: (l, 0, j)),
        out_shape=jax.ShapeDtypeStruct((depth, bsz, n), F32),
        compiler_params=_params("parallel", "parallel"),
        name="ada_mod",
    )(c, w_ada, b_ada.reshape(depth, 1, n))


def _inproj_kernel(x_ref, mod_ref, w_ref, wg_ref, wgt_ref, proj_ref, gate_ref, gatet_ref, u_scr):
    j = pl.program_id(2)

    @pl.when(j == 0)
    def _():
        u = (x_ref[...] * (1.0 + mod_ref[1:2, :]) + mod_ref[0:1, :]).astype(BF16)
        u_scr[...] = u
        gate_ref[...] = jnp.dot(u, wg_ref[...], preferred_element_type=F32)
        for ci in range(u.shape[0] // CHUNK):
            uc = u[ci * CHUNK:(ci + 1) * CHUNK]
            gatet_ref[ci] = lax.dot_general(wgt_ref[...], uc, (((1,), (1,)), ((), ())), preferred_element_type=F32)

    proj_ref[...] = jnp.dot(u_scr[...], w_ref[...], preferred_element_type=F32).astype(BF16)


def _in_proj(x, mod_l, w_main, w_gate, w_gate_t):
    bsz, s, d = x.shape
    tm = min(1024, s)
    tn = 1536
    return pl.pallas_call(
        _inproj_kernel,
        grid=(bsz, s // tm, N_MAIN // tn),
        in_specs=[
            pl.BlockSpec((None, tm, d), lambda b, i, j: (b, i, 0)),
            pl.BlockSpec((None, 6, d), lambda b, i, j: (b, 0, 0)),
            pl.BlockSpec((d, tn), lambda b, i, j: (0, j)),
            pl.BlockSpec((d, LANES), lambda b, i, j: (0, 0)),
            pl.BlockSpec((GATE_LANES, d), lambda b, i, j: (0, 0)),
        ],
        out_specs=[
            pl.BlockSpec((None, tm, tn), lambda b, i, j: (b, i, j)),
            pl.BlockSpec((None, tm, LANES), lambda b, i, j: (b, i, 0)),
            pl.BlockSpec((None, tm // CHUNK, GATE_LANES, CHUNK), lambda b, i, j: (b, i, 0, 0)),
        ],
        out_shape=[
            jax.ShapeDtypeStruct((bsz, s, N_MAIN), BF16),
            jax.ShapeDtypeStruct((bsz, s, LANES), F32),
            jax.ShapeDtypeStruct((bsz, s // CHUNK, GATE_LANES, CHUNK), F32),
        ],
        scratch_shapes=[pltpu.VMEM((tm, d), BF16)],
        compiler_params=_params("parallel", "parallel", "arbitrary"),
        name="in_proj",
    )(x, mod_l, w_main, w_gate, w_gate_t)


def _attn_kernel(sink_ref, q_ref, kp_ref, kc_ref, vp_ref, vc_ref, o_ref, *, tq):
    i = pl.program_id(1)
    nsub = tq // WINDOW
    hd, grp, kvh = ATT_HEAD_DIM, ATT_GROUP, ATT_KV_HEADS
    assert 2 * hd == LANES and grp == 4
    kfull = jnp.concatenate([kp_ref[...], kc_ref[...]], axis=0).astype(F32)
    vfull = jnp.concatenate([vp_ref[...], vc_ref[...]], axis=0).astype(F32)
    lane = lax.broadcasted_iota(jnp.int32, (kfull.shape[0], LANES), 1)
    lo_mask = lane < hd

    def lo_hi(full, kv):
        blk = full[:, (kv // 2) * LANES:(kv // 2 + 1) * LANES]
        swapped = pltpu.roll(blk, hd, axis=1)
        src_lo, src_hi = (blk, swapped) if kv % 2 == 0 else (swapped, blk)
        return jnp.where(lo_mask, src_lo, 0.0).astype(BF16), jnp.where(lo_mask, 0.0, src_hi).astype(BF16)

    qi = lax.broadcasted_iota(jnp.int32, (WINDOW, 2 * WINDOW), 0)
    si = lax.broadcasted_iota(jnp.int32, (WINDOW, 2 * WINDOW), 1)
    diff = qi + WINDOW - si
    band = (diff >= 0) & (diff < WINDOW)
    band0 = band & ((si >= WINDOW) | (i > 0))
    neg_inf = jnp.float32(-jnp.inf)

    kv_lo_hi = [(lo_hi(kfull, kv), lo_hi(vfull, kv)) for kv in range(kvh)]
    units = [(kv, s) for kv in range(kvh) for s in range(nsub)]

    def scores(kv, s):
        c0 = kv * grp * hd
        rows = slice(s * WINDOW, (s + 1) * WINDOW)
        band_rows = slice(s * WINDOW, (s + 2) * WINDOW)
        (k_lo, k_hi), _ = kv_lo_hi[kv]
        q2 = jnp.concatenate([q_ref[rows, c0:c0 + LANES], q_ref[rows, c0 + LANES:c0 + 2 * LANES]], axis=0)
        sc_lo = lax.dot_general(q2, k_lo[band_rows], (((1,), (1,)), ((), ())), preferred_element_type=F32)
        sc_hi = lax.dot_general(q2, k_hi[band_rows], (((1,), (1,)), ((), ())), preferred_element_type=F32)
        return sc_lo[:WINDOW], sc_hi[:WINDOW], sc_lo[WINDOW:], sc_hi[WINDOW:]

    sc_next = scores(*units[0])
    for n, (kv, s) in enumerate(units):
        sc_cur = sc_next
        if n + 1 < len(units):
            sc_next = scores(*units[n + 1])
        c0 = kv * grp * hd
        rows = slice(s * WINDOW, (s + 1) * WINDOW)
        band_rows = slice(s * WINDOW, (s + 2) * WINDOW)
        _, (v_lo, v_hi) = kv_lo_hi[kv]
        valid = band0 if s == 0 else band
        probs = []
        for g, sc_g in enumerate(sc_cur):
            sg = jnp.where(valid, sc_g, neg_inf)
            sink = sink_ref[kv * grp + g]
            m = jnp.maximum(jnp.max(sg, axis=-1, keepdims=True), sink)
            p = jnp.exp(sg - m)
            denom = jnp.sum(p, axis=-1, keepdims=True) + jnp.exp(sink - m)
            probs.append((p / denom).astype(BF16))
        pcat = jnp.concatenate([jnp.concatenate(probs[0:2], axis=1), jnp.concatenate(probs[2:4], axis=1)], axis=0)
        vcat = jnp.concatenate([v_lo[band_rows], v_hi[band_rows]], axis=0)
        o = jnp.dot(pcat, vcat, preferred_element_type=F32)
        o_ref[rows, c0:c0 + LANES] = o[:WINDOW].astype(BF16)
        o_ref[rows, c0 + LANES:c0 + 2 * LANES] = o[WINDOW:].astype(BF16)


def _attention(proj, sinks):
    bsz, s, _ = proj.shape
    tq = min(512, s)
    sub = tq // WINDOW
    kern = functools.partial(_attn_kernel, tq=tq)
    return pl.pallas_call(
        kern,
        grid=(bsz, s // tq),
        in_specs=[
            pl.BlockSpec(memory_space=pltpu.SMEM),
            pl.BlockSpec((None, tq, ATT_Q_W), lambda b, i: (b, i, COLBLK_Q)),
            pl.BlockSpec((None, WINDOW, ATT_KV_W), lambda b, i: (b, jnp.maximum(i * sub - 1, 0), COLBLK_K)),
            pl.BlockSpec((None, tq, ATT_KV_W), lambda b, i: (b, i, COLBLK_K)),
            pl.BlockSpec((None, WINDOW, ATT_KV_W), lambda b, i: (b, jnp.maximum(i * sub - 1, 0), COLBLK_V)),
            pl.BlockSpec((None, tq, ATT_KV_W), lambda b, i: (b, i, COLBLK_V)),
        ],
        out_specs=pl.BlockSpec((None, tq, ATT_Q_W), lambda b, i: (b, i, 0)),
        out_shape=jax.ShapeDtypeStruct((bsz, s, ATT_Q_W), BF16),
        compiler_params=_params("parallel", "parallel"),
        name="swa_attention",
    )(sinks, proj, proj, proj, proj, proj)


def _gdn_kernel(dq_ref, dk_ref, dv_ref, z_ref, gate_ref, gatet_ref, convw_ref, acol_ref, dtcol_ref,
                arow_ref, dtrow_ref, nw_ref, o_ref, xpad, state, u_s, w_s, qdec_s, kdec_s, intra_s, last_s, *, ts):
    t = pl.program_id(1)
    nh, hd = DN_HEADS, DN_HEAD_DIM
    halo = 8

    @pl.when(t == 0)
    def _():
        state[...] = jnp.zeros_like(state)
        for a in range(3):
            xpad[a, 0:halo, :] = jnp.zeros((halo, DN_W), F32)

    @pl.when(t > 0)
    def _():
        for a in range(3):
            xpad[a, 0:halo, :] = xpad[a, ts:ts + halo, :]

    for a, ref in enumerate((dq_ref, dk_ref, dv_ref)):
        xpad[a, halo:halo + ts, :] = ref[...].astype(F32)

    ri = lax.broadcasted_iota(jnp.int32, (CHUNK, CHUNK), 0)
    ci = lax.broadcasted_iota(jnp.int32, (CHUNK, CHUNK), 1)
    causal = ri >= ci
    strict = ri > ci
    tri_lo = causal.astype(F32)
    tri_up = (ri <= ci).astype(F32)
    eye = (ri == ci).astype(F32)
    neg_inf = jnp.float32(-jnp.inf)

    heads = range(nh)
    hsl = [slice(h * hd, (h + 1) * hd) for h in heads]

    def prep(c):
        r0 = pl.multiple_of(c * CHUNK, CHUNK)
        gcols = gate_ref[pl.ds(r0, CHUNK), :]
        beta_cols = jax.nn.sigmoid(gcols)
        g_cols = -jnp.exp(acol_ref[...]) * jax.nn.softplus(gcols + dtcol_ref[...])
        gc_cols = _dot_f32(tri_lo, g_cols)
        grows = gatet_ref[c]
        g_rows = -jnp.exp(arow_ref[...]) * jax.nn.softplus(grows + dtrow_ref[...])
        gc_rows = _dot_f32(g_rows, tri_up)

        qs, ks, vs = [], [], []
        for h in heads:
            xs = []
            for a in range(3):
                cs = slice(a * DN_W + h * hd, a * DN_W + (h + 1) * hd)
                win = xpad[a, pl.ds(r0, CHUNK + halo), hsl[h]]
                acc = win[halo:halo + CHUNK] * convw_ref[CONV_K - 1:CONV_K, cs]
                for jj in range(CONV_K - 1):
                    sh = CONV_K - 1 - jj
                    acc = acc + win[halo - sh:halo - sh + CHUNK] * convw_ref[jj:jj + 1, cs]
                xs.append(_silu(acc))
            q, k, v = xs
            qs.append(q * lax.rsqrt(jnp.sum(q * q, axis=-1, keepdims=True) + RMS_EPS) * (hd ** -0.5))
            ks.append(k * lax.rsqrt(jnp.sum(k * k, axis=-1, keepdims=True) + RMS_EPS))
            vs.append(v)

        gcbs = [jnp.broadcast_to(gc_cols[:, nh + h:nh + h + 1], (CHUNK, hd)) for h in heads]
        gc_lasts = [gc_rows[nh + h:nh + h + 1, CHUNK - 1:CHUNK] for h in heads]
        decays = [jnp.exp(jnp.where(causal, gcbs[h][:, :CHUNK] - gc_rows[nh + h:nh + h + 1, :], neg_inf)) for h in heads]
        egcs = [jnp.exp(gcbs[h]) for h in heads]
        betas = [jnp.broadcast_to(beta_cols[:, h:h + 1], (CHUNK, hd)) for h in heads]
        kbs = [ks[h] * betas[h] for h in heads]
        vbs = [vs[h] * betas[h] for h in heads]

        kkqk = [_dot_nt(jnp.concatenate([kbs[h], qs[h]], axis=0), ks[h]) for h in heads]
        yield
        for h in heads:
            intra_s[h] = (kkqk[h][CHUNK:] * decays[h]).astype(BF16)
            qdec_s[h] = (qs[h] * egcs[h]).astype(BF16)
            kdec_s[h] = (ks[h] * jnp.exp(gc_lasts[h] - gcbs[h])).astype(BF16)
            last_s[h:h + 1, :] = jnp.broadcast_to(jnp.exp(gc_lasts[h]), (1, hd))
        ms = [-jnp.where(strict, kkqk[h][:CHUNK] * decays[h], 0.0) for h in heads]
        xinvs = [eye + ms[h] for h in heads]
        ms = [_dot(ms[h], ms[h]) for h in heads]
        yield
        npow = 2
        while npow < CHUNK:
            if npow * 2 < CHUNK:
                both = [_dot(jnp.concatenate([xinvs[h], ms[h]], axis=0), ms[h]) for h in heads]
                yield
                xinvs = [xinvs[h] + both[h][:CHUNK] for h in heads]
                ms = [both[h][CHUNK:] for h in heads]
            else:
                tails = [_dot(xinvs[h], ms[h]) for h in heads]
                yield
                xinvs = [xinvs[h] + tails[h] for h in heads]
            npow *= 2
        sols = [_dot(xinvs[h], jnp.concatenate([vbs[h], kbs[h] * egcs[h]], axis=1)) for h in heads]
        yield
        for h in heads:
            u_s[h] = sols[h][:, :hd]
            w_s[h] = sols[h][:, hd:].astype(BF16)

    def recur(c):
        r0 = pl.multiple_of(c * CHUNK, CHUNK)
        sts = [state[h] for h in heads]
        us = [u_s[h] for h in heads]
        kdecs = [kdec_s[h] for h in heads]
        intras = [intra_s[h] for h in heads]
        lasts = [last_s[h:h + 1, :] for h in heads]
        ws_qs = [_dot(jnp.concatenate([w_s[h], qdec_s[h]], axis=0), sts[h]) for h in heads]
        yield
        v_news = [us[h] - ws_qs[h][:CHUNK] for h in heads]
        upds = [_dot_tn(kdecs[h], v_news[h]) for h in heads]
        outs = [ws_qs[h][CHUNK:] + _dot(intras[h], v_news[h]) for h in heads]
        yield
        for h in heads:
            state[h] = sts[h] * lasts[h] + upds[h]
        for h in heads:
            o = outs[h]
            o = o * lax.rsqrt(jnp.mean(o * o, axis=-1, keepdims=True) + RMS_EPS) * nw_ref[...]
            o = o * _silu(z_ref[pl.ds(r0, CHUNK), hsl[h]].astype(F32))
            o_ref[pl.ds(r0, CHUNK), hsl[h]] = o.astype(BF16)

    def run_interleaved(*gens):
        live = list(gens)
        while live:
            for g in list(live):
                try:
                    next(g)
                except StopIteration:
                    live.remove(g)

    nchunk = ts // CHUNK
    run_interleaved(prep(0))

    def chunk_body(c, carry):
        run_interleaved(recur(c), prep(c + 1))
        return carry

    lax.fori_loop(0, nchunk - 1, chunk_body, 0)
    run_interleaved(recur(nchunk - 1))


def _gated_deltanet(proj, gate, gate_t, conv_w, a_log, dt_bias, norm_w):
    bsz, s, _ = proj.shape
    ts = min(512, s)
    nh = DN_HEADS
    zpad = jnp.zeros((nh,), F32)
    acol = jnp.concatenate([zpad, a_log, jnp.zeros((LANES - 2 * nh,), F32)]).reshape(1, LANES)
    dtcol = jnp.concatenate([zpad, dt_bias, jnp.zeros((LANES - 2 * nh,), F32)]).reshape(1, LANES)
    arow = jnp.broadcast_to(jnp.concatenate([zpad, a_log]).reshape(GATE_LANES, 1), (GATE_LANES, CHUNK))
    dtrow = jnp.broadcast_to(jnp.concatenate([zpad, dt_bias]).reshape(GATE_LANES, 1), (GATE_LANES, CHUNK))
    kern = functools.partial(_gdn_kernel, ts=ts)
    const = lambda b, t: (0, 0)
    return pl.pallas_call(
        kern,
        grid=(bsz, s // ts),
        in_specs=[
            pl.BlockSpec((None, ts, DN_W), lambda b, t: (b, t, COLBLK_DQ)),
            pl.BlockSpec((None, ts, DN_W), lambda b, t: (b, t, COLBLK_DK)),
            pl.BlockSpec((None, ts, DN_W), lambda b, t: (b, t, COLBLK_DV)),
            pl.BlockSpec((None, ts, DN_W), lambda b, t: (b, t, COLBLK_Z)),
            pl.BlockSpec((None, ts, LANES), lambda b, t: (b, t, 0)),
            pl.BlockSpec((None, ts // CHUNK, GATE_LANES, CHUNK), lambda b, t: (b, t, 0, 0)),
            pl.BlockSpec((CONV_K, 3 * DN_W), const),
            pl.BlockSpec((1, LANES), const),
            pl.BlockSpec((1, LANES), const),
            pl.BlockSpec((GATE_LANES, CHUNK), const),
            pl.BlockSpec((GATE_LANES, CHUNK), const),
            pl.BlockSpec((1, DN_HEAD_DIM), const),
        ],
        out_specs=pl.BlockSpec((None, ts, DN_W), lambda b, t: (b, t, 0)),
        out_shape=jax.ShapeDtypeStruct((bsz, s, DN_W), BF16),
        scratch_shapes=[
            pltpu.VMEM((3, ts + 8, DN_W), F32),
            pltpu.VMEM((DN_HEADS, DN_HEAD_DIM, DN_HEAD_DIM), F32),
            pltpu.VMEM((DN_HEADS, CHUNK, DN_HEAD_DIM), F32),
            pltpu.VMEM((DN_HEADS, CHUNK, DN_HEAD_DIM), BF16),
            pltpu.VMEM((DN_HEADS, CHUNK, DN_HEAD_DIM), BF16),
            pltpu.VMEM((DN_HEADS, CHUNK, DN_HEAD_DIM), BF16),
            pltpu.VMEM((DN_HEADS, CHUNK, CHUNK), BF16),
            pltpu.VMEM((DN_HEADS, DN_HEAD_DIM), F32),
        ],
        compiler_params=_params("parallel", "arbitrary"),
        name="gated_deltanet",
    )(proj, proj, proj, proj, gate, gate_t, conv_w, acol, dtcol, arow, dtrow, norm_w.reshape(1, DN_HEAD_DIM))


def _mix_kernel(att_ref, gdn_ref, ga_ref, gb_ref, x_ref, mod_ref, woa_ref, wob_ref, wout_ref, g_ref, b_ref, o_ref):
    y_a = jnp.dot(att_ref[...], woa_ref[...], preferred_element_type=F32)
    y_b = jnp.dot(gdn_ref[...], wob_ref[...], preferred_element_type=F32)
    merged = jax.nn.sigmoid(ga_ref[...].astype(F32)) * y_a + jax.nn.sigmoid(gb_ref[...].astype(F32)) * y_b
    mixed = jnp.dot(merged.astype(BF16), wout_ref[...], preferred_element_type=F32)
    o_ref[...] = _layer_norm(ALPHA * x_ref[...] + (1.0 + mod_ref[2:3, :]) * mixed, g_ref[...], b_ref[...])


def _mix(att, gdn, proj, x, mod_l, w_oa, w_ob, w_out, ln_g, ln_b):
    bsz, s, d = x.shape
    tm = min(512, s)
    row = lambda b, i: (b, i, 0)
    const = lambda b, i: (0, 0)
    return pl.pallas_call(
        _mix_kernel,
        grid=(bsz, s // tm),
        in_specs=[
            pl.BlockSpec((None, tm, d), row),
            pl.BlockSpec((None, tm, d), row),
            pl.BlockSpec((None, tm, d), lambda b, i: (b, i, COLBLK_GA)),
            pl.BlockSpec((None, tm, d), lambda b, i: (b, i, COLBLK_GB)),
            pl.BlockSpec((None, tm, d), row),
            pl.BlockSpec((None, 6, d), lambda b, i: (b, 0, 0)),
            pl.BlockSpec((d, d), const),
            pl.BlockSpec((d, d), const),
            pl.BlockSpec((d, d), const),
            pl.BlockSpec((1, d), const),
            pl.BlockSpec((1, d), const),
        ],
        out_specs=pl.BlockSpec((None, tm, d), row),
        out_shape=jax.ShapeDtypeStruct((bsz, s, d), F32),
        compiler_params=_params("parallel", "parallel"),
        name="mix_out_ln",
    )(att, gdn, proj, proj, x, mod_l, w_oa, w_ob, w_out, ln_g.reshape(1, d), ln_b.reshape(1, d))


def _ffn_kernel(x_ref, mod_ref, w1_ref, b1_ref, w2_ref, b2_ref, g_ref, b_ref, o_ref, u_scr, acc):
    f = pl.program_id(2)

    @pl.when(f == 0)
    def _():
        u_scr[...] = (x_ref[...] * (1.0 + mod_ref[4:5, :]) + mod_ref[3:4, :]).astype(BF16)
        acc[...] = jnp.zeros_like(acc)

    h = jnp.dot(u_scr[...], w1_ref[...], preferred_element_type=F32) + b1_ref[...]
    h = jnp.square(jnp.maximum(h, 0.0))
    acc[...] += jnp.dot(h.astype(BF16), w2_ref[...], preferred_element_type=F32)

    @pl.when(f == pl.num_programs(2) - 1)
    def _():
        y = acc[...] + b2_ref[...]
        o_ref[...] = _layer_norm(ALPHA * x_ref[...] + (1.0 + mod_ref[5:6, :]) * y, g_ref[...], b_ref[...])


def _ffn(x, mod_l, w1, b1, w2, b2, ln_g, ln_b):
    bsz, s, d = x.shape
    dff = w1.shape[1]
    tm = min(1024, s)
    tf = 1024
    row = lambda b, i, f: (b, i, 0)
    const = lambda b, i, f: (0, 0)
    return pl.pallas_call(
        _ffn_kernel,
        grid=(bsz, s // tm, dff // tf),
        in_specs=[
            pl.BlockSpec((None, tm, d), row),
            pl.BlockSpec((None, 6, d), lambda b, i, f: (b, 0, 0)),
            pl.BlockSpec((d, tf), lambda b, i, f: (0, f)),
            pl.BlockSpec((1, tf), lambda b, i, f: (0, f)),
            pl.BlockSpec((tf, d), lambda b, i, f: (f, 0)),
            pl.BlockSpec((1, d), const),
            pl.BlockSpec((1, d), const),
            pl.BlockSpec((1, d), const),
        ],
        out_specs=pl.BlockSpec((None, tm, d), row),
        out_shape=jax.ShapeDtypeStruct((bsz, s, d), F32),
        scratch_shapes=[pltpu.VMEM((tm, d), BF16), pltpu.VMEM((tm, d), F32)],
        compiler_params=_params("parallel", "parallel", "arbitrary"),
        name="ffn_ln",
    )(x, mod_l, w1, b1.reshape(1, dff), w2, b2.reshape(1, d), ln_g.reshape(1, d), ln_b.reshape(1, d))


def kernel(x, c, w_ada, b_ada, w_in, conv_w, a_log, dt_bias, sinks, dn_norm_w, w_oa, w_ob, w_out,
           ln1_g, ln1_b, w_ff1, b_ff1, w_ff2, b_ff2, ln2_g, ln2_b):
    depth = w_ada.shape[0]
    bsz, s, d = x.shape
    mod = _ada_mod(c, w_ada, b_ada).reshape(depth, bsz, 6, d)
    gate_lo = ATT_Q_W + 2 * ATT_KV_W + 4 * DN_W
    gate_hi = gate_lo + GATE_LANES
    w_gate_f32 = w_in[:, :, gate_lo:gate_hi]
    w_gate_all = jnp.pad(w_gate_f32, ((0, 0), (0, 0), (0, LANES - GATE_LANES))).astype(BF16)
    w_gate_t_all = jnp.swapaxes(w_gate_f32, 1, 2).astype(BF16)
    for l in range(depth):
        wl = w_in[l]
        w_main = jnp.concatenate(
            [wl[:, :ATT_Q_W] * (ATT_HEAD_DIM ** -0.5), wl[:, ATT_Q_W + 2 * ATT_KV_W:gate_lo], wl[:, gate_hi:],
             wl[:, ATT_Q_W:ATT_Q_W + 2 * ATT_KV_W]], axis=1).astype(BF16)
        proj, gate, gate_t = _in_proj(x, mod[l], w_main, w_gate_all[l], w_gate_t_all[l])
        att = _attention(proj, sinks[l])
        gdn = _gated_deltanet(proj, gate, gate_t, conv_w[l], a_log[l], dt_bias[l], dn_norm_w[l])
        x = _mix(att, gdn, proj, x, mod[l], w_oa[l].astype(BF16), w_ob[l].astype(BF16), w_out[l].astype(BF16),
                 ln1_g[l], ln1_b[l])
        x = _ffn(x, mod[l], w_ff1[l].astype(BF16), b_ff1[l], w_ff2[l].astype(BF16), b_ff2[l], ln2_g[l], ln2_b[l])
    return x
```

```python
import functools

import jax
import jax.numpy as jnp
from jax import lax
from jax.experimental import pallas as pl
from jax.experimental.pallas import tpu as pltpu

F32 = jnp.float32
BF16 = jnp.bfloat16

D_MODEL = 1024
DEPTH = 4
ATT_HEADS = 16
ATT_KV_HEADS = 4
ATT_GROUP = ATT_HEADS // ATT_KV_HEADS
ATT_HEAD_DIM = 64
WINDOW = 128
DN_HEADS = 8
DN_HEAD_DIM = 128
CONV_K = 4
CHUNK = 64
D_FF = 4 * D_MODEL
ATT_Q_W = ATT_HEADS * ATT_HEAD_DIM
ATT_KV_W = ATT_KV_HEADS * ATT_HEAD_DIM
DN_W = DN_HEADS * DN_HEAD_DIM
ALPHA = (2 * DEPTH) ** 0.25
LN_EPS = 1e-5
RMS_EPS = 1e-6

LANES = 128
SUBLANES = 8
GATE_LANES = 2 * DN_HEADS
N_MAIN = ATT_Q_W + 4 * DN_W + 2 * D_MODEL + 2 * ATT_KV_W
COLBLK_Q, COLBLK_DQ, COLBLK_DK, COLBLK_DV, COLBLK_Z, COLBLK_GA, COLBLK_GB = range(7)
COLBLK_K = (ATT_Q_W + 4 * DN_W + 2 * D_MODEL) // ATT_KV_W
COLBLK_V = COLBLK_K + 1
VMEM_LIMIT = 48 * 1024 * 1024


def _params(*sem):
    return pltpu.CompilerParams(dimension_semantics=sem, vmem_limit_bytes=VMEM_LIMIT)


def _dot(a, b):
    return jnp.dot(a.astype(BF16), b.astype(BF16), preferred_element_type=F32)


def _dot_nt(a, b):
    return lax.dot_general(a.astype(BF16), b.astype(BF16), (((1,), (1,)), ((), ())), preferred_element_type=F32)


def _dot_tn(a, b):
    return lax.dot_general(a.astype(BF16), b.astype(BF16), (((0,), (0,)), ((), ())), preferred_element_type=F32)


def _dot_f32(a, b):
    return jnp.dot(a, b, precision=lax.Precision.HIGHEST, preferred_element_type=F32)


def _silu(t):
    return t * jax.nn.sigmoid(t)


def _layer_norm(t, g, b):
    mu = jnp.mean(t, axis=-1, keepdims=True)
    tc = t - mu
    var = jnp.mean(tc * tc, axis=-1, keepdims=True)
    return tc * lax.rsqrt(var + LN_EPS) * g + b


def _ada_kernel(c_ref, w_ref, b_ref, o_ref):
    ca = _silu(c_ref[...])
    o_ref[...] = _dot(ca, w_ref[...]) + b_ref[...]


def _ada_mod(c, w_ada, b_ada):
    depth, d, n = w_ada.shape
    bsz = c.shape[0]
    tn = 1536
    return pl.pallas_call(
        _ada_kernel,
        grid=(depth, n // tn),
        in_specs=[
            pl.BlockSpec((bsz, d), lambda l, j: (0, 0)),
            pl.BlockSpec((None, d, tn), lambda l, j: (l, 0, j)),
            pl.BlockSpec((None, 1, tn), lambda l, j: (l, 0, j)),
        ],
        out_specs=pl.BlockSpec((None, bsz, tn), lambda l, j: (l, 0, j)),
        out_shape=jax.ShapeDtypeStruct((depth, bsz, n), F32),
        compiler_params=_params("parallel", "parallel"),
        name="ada_mod",
    )(c, w_ada, b_ada.reshape(depth, 1, n))


def _inproj_kernel(x_ref, mod_ref, w_ref, wg_ref, wgt_ref, proj_ref, gate_ref, gatet_ref, u_scr):
    j = pl.program_id(2)

    @pl.when(j == 0)
    def _():
        u = (x_ref[...] * (1.0 + mod_ref[1:2, :]) + mod_ref[0:1, :]).astype(BF16)
        u_scr[...] = u
        gate_ref[...] = jnp.dot(u, wg_ref[...], preferred_element_type=F32)
        for ci in range(u.shape[0] // CHUNK):
            uc = u[ci * CHUNK:(ci + 1) * CHUNK]
            gatet_ref[ci] = lax.dot_general(wgt_ref[...], uc, (((1,), (1,)), ((), ())), preferred_element_type=F32)

    proj_ref[...] = jnp.dot(u_scr[...], w_ref[...], preferred_element_type=F32).astype(BF16)


def _in_proj(x, mod_l, w_main, w_gate, w_gate_t):
    bsz, s, d = x.shape
    tm = min(1024, s)
    tn = 1536
    return pl.pallas_call(
        _inproj_kernel,
        grid=(bsz, s // tm, N_MAIN // tn),
        in_specs=[
            pl.BlockSpec((None, tm, d), lambda b, i, j: (b, i, 0)),
            pl.BlockSpec((None, 6, d), lambda b, i, j: (b, 0, 0)),
            pl.BlockSpec((d, tn), lambda b, i, j: (0, j)),
            pl.BlockSpec((d, LANES), lambda b, i, j: (0, 0)),
            pl.BlockSpec((GATE_LANES, d), lambda b, i, j: (0, 0)),
        ],
        out_specs=[
            pl.BlockSpec((None, tm, tn), lambda b, i, j: (b, i, j)),
            pl.BlockSpec((None, tm, LANES), lambda b, i, j: (b, i, 0)),
            pl.BlockSpec((None, tm // CHUNK, GATE_LANES, CHUNK), lambda b, i, j: (b, i, 0, 0)),
        ],
        out_shape=[
            jax.ShapeDtypeStruct((bsz, s, N_MAIN), BF16),
            jax.ShapeDtypeStruct((bsz, s, LANES), F32),
            jax.ShapeDtypeStruct((bsz, s // CHUNK, GATE_LANES, CHUNK), F32),
        ],
        scratch_shapes=[pltpu.VMEM((tm, d), BF16)],
        compiler_params=_params("parallel", "parallel", "arbitrary"),
        name="in_proj",
    )(x, mod_l, w_main, w_gate, w_gate_t)


def _attn_kernel(sink_ref, q_ref, kp_ref, kc_ref, vp_ref, vc_ref, o_ref, *, tq):
    i = pl.program_id(1)
    nsub = tq // WINDOW
    hd, grp, kvh = ATT_HEAD_DIM, ATT_GROUP, ATT_KV_HEADS
    assert 2 * hd == LANES and grp == 4
    kfull = jnp.concatenate([kp_ref[...], kc_ref[...]], axis=0).astype(F32)
    vfull = jnp.concatenate([vp_ref[...], vc_ref[...]], axis=0).astype(F32)
    lane = lax.broadcasted_iota(jnp.int32, (kfull.shape[0], LANES), 1)
    lo_mask = lane < hd

    def lo_hi(full, kv):
        blk = full[:, (kv // 2) * LANES:(kv // 2 + 1) * LANES]
        swapped = pltpu.roll(blk, hd, axis=1)
        src_lo, src_hi = (blk, swapped) if kv % 2 == 0 else (swapped, blk)
        return jnp.where(lo_mask, src_lo, 0.0).astype(BF16), jnp.where(lo_mask, 0.0, src_hi).astype(BF16)

    qi = lax.broadcasted_iota(jnp.int32, (WINDOW, 2 * WINDOW), 0)
    si = lax.broadcasted_iota(jnp.int32, (WINDOW, 2 * WINDOW), 1)
    diff = qi + WINDOW - si
    band = (diff >= 0) & (diff < WINDOW)
    band0 = band & ((si >= WINDOW) | (i > 0))
    neg_inf = jnp.float32(-jnp.inf)

    kv_lo_hi = [(lo_hi(kfull, kv), lo_hi(vfull, kv)) for kv in range(kvh)]
    units = [(kv, s) for kv in range(kvh) for s in range(nsub)]

    def scores(kv, s):
        c0 = kv * grp * hd
        rows = slice(s * WINDOW, (s + 1) * WINDOW)
        band_rows = slice(s * WINDOW, (s + 2) * WINDOW)
        (k_lo, k_hi), _ = kv_lo_hi[kv]
        q2 = jnp.concatenate([q_ref[rows, c0:c0 + LANES], q_ref[rows, c0 + LANES:c0 + 2 * LANES]], axis=0)
        sc_lo = lax.dot_general(q2, k_lo[band_rows], (((1,), (1,)), ((), ())), preferred_element_type=F32)
        sc_hi = lax.dot_general(q2, k_hi[band_rows], (((1,), (1,)), ((), ())), preferred_element_type=F32)
        return sc_lo[:WINDOW], sc_hi[:WINDOW], sc_lo[WINDOW:], sc_hi[WINDOW:]

    sc_next = scores(*units[0])
    for n, (kv, s) in enumerate(units):
        sc_cur = sc_next
        if n + 1 < len(units):
            sc_next = scores(*units[n + 1])
        c0 = kv * grp * hd
        rows = slice(s * WINDOW, (s + 1) * WINDOW)
        band_rows = slice(s * WINDOW, (s + 2) * WINDOW)
        _, (v_lo, v_hi) = kv_lo_hi[kv]
        valid = band0 if s == 0 else band
        probs = []
        for g, sc_g in enumerate(sc_cur):
            sg = jnp.where(valid, sc_g, neg_inf)
            sink = sink_ref[kv * grp + g]
            m = jnp.maximum(jnp.max(sg, axis=-1, keepdims=True), sink)
            p = jnp.exp(sg - m)
            denom = jnp.sum(p, axis=-1, keepdims=True) + jnp.exp(sink - m)
            probs.append((p / denom).astype(BF16))
        pcat = jnp.concatenate([jnp.concatenate(probs[0:2], axis=1), jnp.concatenate(probs[2:4], axis=1)], axis=0)
        vcat = jnp.concatenate([v_lo[band_rows], v_hi[band_rows]], axis=0)
        o = jnp.dot(pcat, vcat, preferred_element_type=F32)
        o_ref[rows, c0:c0 + LANES] = o[:WINDOW].astype(BF16)
        o_ref[rows, c0 + LANES:c0 + 2 * LANES] = o[WINDOW:].astype(BF16)


def _attention(proj, sinks):
    bsz, s, _ = proj.shape
    tq = min(512, s)
    sub = tq // WINDOW
    kern = functools.partial(_attn_kernel, tq=tq)
    return pl.pallas_call(
        kern,
        grid=(bsz, s // tq),
        in_specs=[
            pl.BlockSpec(memory_space=pltpu.SMEM),
            pl.BlockSpec((None, tq, ATT_Q_W), lambda b, i: (b, i, COLBLK_Q)),
            pl.BlockSpec((None, WINDOW, ATT_KV_W), lambda b, i: (b, jnp.maximum(i * sub - 1, 0), COLBLK_K)),
            pl.BlockSpec((None, tq, ATT_KV_W), lambda b, i: (b, i, COLBLK_K)),
            pl.BlockSpec((None, WINDOW, ATT_KV_W), lambda b, i: (b, jnp.maximum(i * sub - 1, 0), COLBLK_V)),
            pl.BlockSpec((None, tq, ATT_KV_W), lambda b, i: (b, i, COLBLK_V)),
        ],
        out_specs=pl.BlockSpec((None, tq, ATT_Q_W), lambda b, i: (b, i, 0)),
        out_shape=jax.ShapeDtypeStruct((bsz, s, ATT_Q_W), BF16),
        compiler_params=_params("parallel", "parallel"),
        name="swa_attention",
    )(sinks, proj, proj, proj, proj, proj)


def _gdn_kernel(dq_ref, dk_ref, dv_ref, z_ref, gate_ref, gatet_ref, convw_ref, acol_ref, dtcol_ref,
                arow_ref, dtrow_ref, nw_ref, o_ref, xpad, state, u_s, w_s, qdec_s, kdec_s, intra_s, last_s,
                hqdec_s, hkdec_s, hintra_s, hlast_s, hm_s, hrhs_s, *, ts):
    t = pl.program_id(1)
    nh, hd = DN_HEADS, DN_HEAD_DIM
    halo = 8

    @pl.when(t == 0)
    def _():
        state[...] = jnp.zeros_like(state)
        for a in range(3):
            xpad[a, 0:halo, :] = jnp.zeros((halo, DN_W), F32)

    @pl.when(t > 0)
    def _():
        for a in range(3):
            xpad[a, 0:halo, :] = xpad[a, ts:ts + halo, :]

    for a, ref in enumerate((dq_ref, dk_ref, dv_ref)):
        xpad[a, halo:halo + ts, :] = ref[...].astype(F32)

    ri = lax.broadcasted_iota(jnp.int32, (CHUNK, CHUNK), 0)
    ci = lax.broadcasted_iota(jnp.int32, (CHUNK, CHUNK), 1)
    causal = ri >= ci
    strict = ri > ci
    tri_lo = causal.astype(F32)
    tri_up = (ri <= ci).astype(F32)
    eye = (ri == ci).astype(F32)
    neg_inf = jnp.float32(-jnp.inf)

    heads = range(nh)
    hsl = [slice(h * hd, (h + 1) * hd) for h in heads]

    def prep(c):
        sb = (c & 1) * nh
        r0 = pl.multiple_of(c * CHUNK, CHUNK)
        gcols = gate_ref[pl.ds(r0, CHUNK), :]
        beta_cols = jax.nn.sigmoid(gcols)
        g_cols = -jnp.exp(acol_ref[...]) * jax.nn.softplus(gcols + dtcol_ref[...])
        gc_cols = _dot_f32(tri_lo, g_cols)
        grows = gatet_ref[c]
        g_rows = -jnp.exp(arow_ref[...]) * jax.nn.softplus(grows + dtrow_ref[...])
        gc_rows = _dot_f32(g_rows, tri_up)
        yield

        qs, ks, vs = [], [], []
        for h in heads:
            if h and h % 2 == 0:
                yield
            xs = []
            for a in range(3):
                cs = slice(a * DN_W + h * hd, a * DN_W + (h + 1) * hd)
                win = xpad[a, pl.ds(r0, CHUNK + halo), hsl[h]]
                acc = win[halo:halo + CHUNK] * convw_ref[CONV_K - 1:CONV_K, cs]
                for jj in range(CONV_K - 1):
                    sh = CONV_K - 1 - jj
                    acc = acc + win[halo - sh:halo - sh + CHUNK] * convw_ref[jj:jj + 1, cs]
                xs.append(_silu(acc))
            q, k, v = xs
            qs.append(q * lax.rsqrt(jnp.sum(q * q, axis=-1, keepdims=True) + RMS_EPS) * (hd ** -0.5))
            ks.append(k * lax.rsqrt(jnp.sum(k * k, axis=-1, keepdims=True) + RMS_EPS))
            vs.append(v)
        yield

        gcbs = [jnp.broadcast_to(gc_cols[:, nh + h:nh + h + 1], (CHUNK, hd)) for h in heads]
        gc_lasts = [gc_rows[nh + h:nh + h + 1, CHUNK - 1:CHUNK] for h in heads]
        decays = [jnp.exp(jnp.where(causal, gcbs[h][:, :CHUNK] - gc_rows[nh + h:nh + h + 1, :], neg_inf)) for h in heads]
        egcs = [jnp.exp(gcbs[h]) for h in heads]
        betas = [jnp.broadcast_to(beta_cols[:, h:h + 1], (CHUNK, hd)) for h in heads]
        kbs = [ks[h] * betas[h] for h in heads]
        vbs = [vs[h] * betas[h] for h in heads]
        yield

        kkqk = [_dot_nt(jnp.concatenate([kbs[h], qs[h]], axis=0), ks[h]) for h in heads]
        yield
        for h in heads:
            hintra_s[sb + h] = (kkqk[h][CHUNK:] * decays[h]).astype(BF16)
            hqdec_s[sb + h] = (qs[h] * egcs[h]).astype(BF16)
            hkdec_s[sb + h] = (ks[h] * jnp.exp(gc_lasts[h] - gcbs[h])).astype(BF16)
            hlast_s[sb + h] = jnp.broadcast_to(jnp.exp(gc_lasts[h]), (SUBLANES, hd))
            hm_s[sb + h] = -jnp.where(strict, kkqk[h][:CHUNK] * decays[h], 0.0)
            hrhs_s[sb + h] = jnp.concatenate([vbs[h], kbs[h] * egcs[h]], axis=1).astype(BF16)

    def solve(c):
        sb = (c & 1) * nh
        ms = [hm_s[sb + h] for h in heads]
        rhss = [hrhs_s[sb + h] for h in heads]
        for h in heads:
            intra_s[sb + h] = hintra_s[sb + h]
            qdec_s[sb + h] = hqdec_s[sb + h]
            kdec_s[sb + h] = hkdec_s[sb + h]
            last_s[sb + h] = hlast_s[sb + h]
        xinvs = [eye + ms[h] for h in heads]
        ms = [_dot(ms[h], ms[h]) for h in heads]
        yield
        npow = 2
        while npow < CHUNK:
            if npow * 2 < CHUNK:
                both = [_dot(jnp.concatenate([xinvs[h], ms[h]], axis=0), ms[h]) for h in heads]
                yield
                xinvs = [xinvs[h] + both[h][:CHUNK] for h in heads]
                ms = [both[h][CHUNK:] for h in heads]
            else:
                tails = [_dot(xinvs[h], ms[h]) for h in heads]
                yield
                xinvs = [xinvs[h] + tails[h] for h in heads]
            npow *= 2
        sols = [_dot(xinvs[h], rhss[h]) for h in heads]
        yield
        for h in heads:
            u_s[sb + h] = sols[h][:, :hd]
            w_s[sb + h] = sols[h][:, hd:].astype(BF16)

    def recur(c):
        r0 = pl.multiple_of(c * CHUNK, CHUNK)
        sb = (c & 1) * nh
        sts = [state[h] for h in heads]
        us = [u_s[sb + h] for h in heads]
        kdecs = [kdec_s[sb + h] for h in heads]
        intras = [intra_s[sb + h] for h in heads]
        lasts = [last_s[sb + h][0:1, :] for h in heads]
        ws_qs = [_dot(jnp.concatenate([w_s[sb + h], qdec_s[sb + h]], axis=0), sts[h]) for h in heads]
        yield
        v_news = [us[h] - ws_qs[h][:CHUNK] for h in heads]
        upds = [_dot_tn(kdecs[h], v_news[h]) for h in heads]
        outs = [ws_qs[h][CHUNK:] + _dot(intras[h], v_news[h]) for h in heads]
        yield
        for h in heads:
            state[h] = sts[h] * lasts[h] + upds[h]
        for h in heads:
            o = outs[h]
            o = o * lax.rsqrt(jnp.mean(o * o, axis=-1, keepdims=True) + RMS_EPS) * nw_ref[...]
            o = o * _silu(z_ref[pl.ds(r0, CHUNK), hsl[h]].astype(F32))
            o_ref[pl.ds(r0, CHUNK), hsl[h]] = o.astype(BF16)

    def run_interleaved(*gens):
        live = list(gens)
        while live:
            for g in list(live):
                try:
                    next(g)
                except StopIteration:
                    live.remove(g)

    nchunk = ts // CHUNK
    assert nchunk >= 2
    run_interleaved(prep(0))
    run_interleaved(solve(0), prep(1))

    def chunk_body(c, carry):
        run_interleaved(recur(c), solve(c + 1), prep(c + 2))
        return carry

    lax.fori_loop(0, nchunk - 2, chunk_body, 0)
    run_interleaved(recur(nchunk - 2), solve(nchunk - 1))
    run_interleaved(recur(nchunk - 1))


def _gated_deltanet(proj, gate, gate_t, conv_w, a_log, dt_bias, norm_w):
    bsz, s, _ = proj.shape
    ts = min(512, s)
    nh = DN_HEADS
    zpad = jnp.zeros((nh,), F32)
    acol = jnp.concatenate([zpad, a_log, jnp.zeros((LANES - 2 * nh,), F32)]).reshape(1, LANES)
    dtcol = jnp.concatenate([zpad, dt_bias, jnp.zeros((LANES - 2 * nh,), F32)]).reshape(1, LANES)
    arow = jnp.broadcast_to(jnp.concatenate([zpad, a_log]).reshape(GATE_LANES, 1), (GATE_LANES, CHUNK))
    dtrow = jnp.broadcast_to(jnp.concatenate([zpad, dt_bias]).reshape(GATE_LANES, 1), (GATE_LANES, CHUNK))
    kern = functools.partial(_gdn_kernel, ts=ts)
    const = lambda b, t: (0, 0)
    return pl.pallas_call(
        kern,
        grid=(bsz, s // ts),
        in_specs=[
            pl.BlockSpec((None, ts, DN_W), lambda b, t: (b, t, COLBLK_DQ)),
            pl.BlockSpec((None, ts, DN_W), lambda b, t: (b, t, COLBLK_DK)),
            pl.BlockSpec((None, ts, DN_W), lambda b, t: (b, t, COLBLK_DV)),
            pl.BlockSpec((None, ts, DN_W), lambda b, t: (b, t, COLBLK_Z)),
            pl.BlockSpec((None, ts, LANES), lambda b, t: (b, t, 0)),
            pl.BlockSpec((None, ts // CHUNK, GATE_LANES, CHUNK), lambda b, t: (b, t, 0, 0)),
            pl.BlockSpec((CONV_K, 3 * DN_W), const),
            pl.BlockSpec((1, LANES), const),
            pl.BlockSpec((1, LANES), const),
            pl.BlockSpec((GATE_LANES, CHUNK), const),
            pl.BlockSpec((GATE_LANES, CHUNK), const),
            pl.BlockSpec((1, DN_HEAD_DIM), const),
        ],
        out_specs=pl.BlockSpec((None, ts, DN_W), lambda b, t: (b, t, 0)),
        out_shape=jax.ShapeDtypeStruct((bsz, s, DN_W), BF16),
        scratch_shapes=[
            pltpu.VMEM((3, ts + 8, DN_W), F32),
            pltpu.VMEM((DN_HEADS, DN_HEAD_DIM, DN_HEAD_DIM), F32),
            pltpu.VMEM((2 * DN_HEADS, CHUNK, DN_HEAD_DIM), F32),
            pltpu.VMEM((2 * DN_HEADS, CHUNK, DN_HEAD_DIM), BF16),
            pltpu.VMEM((2 * DN_HEADS, CHUNK, DN_HEAD_DIM), BF16),
            pltpu.VMEM((2 * DN_HEADS, CHUNK, DN_HEAD_DIM), BF16),
            pltpu.VMEM((2 * DN_HEADS, CHUNK, CHUNK), BF16),
            pltpu.VMEM((2 * DN_HEADS, SUBLANES, DN_HEAD_DIM), F32),
            pltpu.VMEM((2 * DN_HEADS, CHUNK, DN_HEAD_DIM), BF16),
            pltpu.VMEM((2 * DN_HEADS, CHUNK, DN_HEAD_DIM), BF16),
            pltpu.VMEM((2 * DN_HEADS, CHUNK, CHUNK), BF16),
            pltpu.VMEM((2 * DN_HEADS, SUBLANES, DN_HEAD_DIM), F32),
            pltpu.VMEM((2 * DN_HEADS, CHUNK, CHUNK), F32),
            pltpu.VMEM((2 * DN_HEADS, CHUNK, 2 * DN_HEAD_DIM), BF16),
        ],
        compiler_params=_params("parallel", "arbitrary"),
        name="gated_deltanet",
    )(proj, proj, proj, proj, gate, gate_t, conv_w, acol, dtcol, arow, dtrow, norm_w.reshape(1, DN_HEAD_DIM))


def _mix_kernel(att_ref, gdn_ref, ga_ref, gb_ref, x_ref, mod_ref, woa_ref, wob_ref, wout_ref, g_ref, b_ref, o_ref):
    y_a = jnp.dot(att_ref[...], woa_ref[...], preferred_element_type=F32)
    y_b = jnp.dot(gdn_ref[...], wob_ref[...], preferred_element_type=F32)
    merged = jax.nn.sigmoid(ga_ref[...].astype(F32)) * y_a + jax.nn.sigmoid(gb_ref[...].astype(F32)) * y_b
    mixed = jnp.dot(merged.astype(BF16), wout_ref[...], preferred_element_type=F32)
    o_ref[...] = _layer_norm(ALPHA * x_ref[...] + (1.0 + mod_ref[2:3, :]) * mixed, g_ref[...], b_ref[...])


def _mix(att, gdn, proj, x, mod_l, w_oa, w_ob, w_out, ln_g, ln_b):
    bsz, s, d = x.shape
    tm = min(512, s)
    row = lambda b, i: (b, i, 0)
    const = lambda b, i: (0, 0)
    return pl.pallas_call(
        _mix_kernel,
        grid=(bsz, s // tm),
        in_specs=[
            pl.BlockSpec((None, tm, d), row),
            pl.BlockSpec((None, tm, d), row),
            pl.BlockSpec((None, tm, d), lambda b, i: (b, i, COLBLK_GA)),
            pl.BlockSpec((None, tm, d), lambda b, i: (b, i, COLBLK_GB)),
            pl.BlockSpec((None, tm, d), row),
            pl.BlockSpec((None, 6, d), lambda b, i: (b, 0, 0)),
            pl.BlockSpec((d, d), const),
            pl.BlockSpec((d, d), const),
            pl.BlockSpec((d, d), const),
            pl.BlockSpec((1, d), const),
            pl.BlockSpec((1, d), const),
        ],
        out_specs=pl.BlockSpec((None, tm, d), row),
        out_shape=jax.ShapeDtypeStruct((bsz, s, d), F32),
        compiler_params=_params("parallel", "parallel"),
        name="mix_out_ln",
    )(att, gdn, proj, proj, x, mod_l, w_oa, w_ob, w_out, ln_g.reshape(1, d), ln_b.reshape(1, d))


def _ffn_kernel(x_ref, mod_ref, w1_ref, b1_ref, w2_ref, b2_ref, g_ref, b_ref, o_ref, u_scr, acc):
    f = pl.program_id(2)

    @pl.when(f == 0)
    def _():
        u_scr[...] = (x_ref[...] * (1.0 + mod_ref[4:5, :]) + mod_ref[3:4, :]).astype(BF16)
        acc[...] = jnp.zeros_like(acc)

    h = jnp.dot(u_scr[...], w1_ref[...], preferred_element_type=F32) + b1_ref[...]
    h = jnp.square(jnp.maximum(h, 0.0))
    acc[...] += jnp.dot(h.astype(BF16), w2_ref[...], preferred_element_type=F32)

    @pl.when(f == pl.num_programs(2) - 1)
    def _():
        y = acc[...] + b2_ref[...]
        o_ref[...] = _layer_norm(ALPHA * x_ref[...] + (1.0 + mod_ref[5:6, :]) * y, g_ref[...], b_ref[...])


def _ffn(x, mod_l, w1, b1, w2, b2, ln_g, ln_b):
    bsz, s, d = x.shape
    dff = w1.shape[1]
    tm = min(1024, s)
    tf = 1024
    row = lambda b, i, f: (b, i, 0)
    const = lambda b, i, f: (0, 0)
    return pl.pallas_call(
        _ffn_kernel,
        grid=(bsz, s // tm, dff // tf),
        in_specs=[
            pl.BlockSpec((None, tm, d), row),
            pl.BlockSpec((None, 6, d), lambda b, i, f: (b, 0, 0)),
            pl.BlockSpec((d, tf), lambda b, i, f: (0, f)),
            pl.BlockSpec((1, tf), lambda b, i, f: (0, f)),
            pl.BlockSpec((tf, d), lambda b, i, f: (f, 0)),
            pl.BlockSpec((1, d), const),
            pl.BlockSpec((1, d), const),
            pl.BlockSpec((1, d), const),
        ],
        out_specs=pl.BlockSpec((None, tm, d), row),
        out_shape=jax.ShapeDtypeStruct((bsz, s, d), F32),
        scratch_shapes=[pltpu.VMEM((tm, d), BF16), pltpu.VMEM((tm, d), F32)],
        compiler_params=_params("parallel", "parallel", "arbitrary"),
        name="ffn_ln",
    )(x, mod_l, w1, b1.reshape(1, dff), w2, b2.reshape(1, d), ln_g.reshape(1, d), ln_b.reshape(1, d))


def kernel(x, c, w_ada, b_ada, w_in, conv_w, a_log, dt_bias, sinks, dn_norm_w, w_oa, w_ob, w_out,
           ln1_g, ln1_b, w_ff1, b_ff1, w_ff2, b_ff2, ln2_g, ln2_b):
    depth = w_ada.shape[0]
    bsz, s, d = x.shape
    mod = _ada_mod(c, w_ada, b_ada).reshape(depth, bsz, 6, d)
    gate_lo = ATT_Q_W + 2 * ATT_KV_W + 4 * DN_W
    gate_hi = gate_lo + GATE_LANES
    w_gate_f32 = w_in[:, :, gate_lo:gate_hi]
    w_gate_all = jnp.pad(w_gate_f32, ((0, 0), (0, 0), (0, LANES - GATE_LANES))).astype(BF16)
    w_gate_t_all = jnp.swapaxes(w_gate_f32, 1, 2).astype(BF16)
    for l in range(depth):
        wl = w_in[l]
        w_main = jnp.concatenate(
            [wl[:, :ATT_Q_W] * (ATT_HEAD_DIM ** -0.5), wl[:, ATT_Q_W + 2 * ATT_KV_W:gate_lo], wl[:, gate_hi:],
             wl[:, ATT_Q_W:ATT_Q_W + 2 * ATT_KV_W]], axis=1).astype(BF16)
        proj, gate, gate_t = _in_proj(x, mod[l], w_main, w_gate_all[l], w_gate_t_all[l])
        att = _attention(proj, sinks[l])
        gdn = _gated_deltanet(proj, gate, gate_t, conv_w[l], a_log[l], dt_bias[l], dn_norm_w[l])
        x = _mix(att, gdn, proj, x, mod[l], w_oa[l].astype(BF16), w_ob[l].astype(BF16), w_out[l].astype(BF16),
                 ln1_g[l], ln1_b[l])
        x = _ffn(x, mod[l], w_ff1[l].astype(BF16), b_ff1[l], w_ff2[l].astype(BF16), b_ff2[l], ln2_g[l], ln2_b[l])
    return x
```

```python
import functools

import jax
import jax.numpy as jnp
from jax import lax
from jax.experimental import pallas as pl
from jax.experimental.pallas import tpu as pltpu

F32 = jnp.float32
BF16 = jnp.bfloat16

D_MODEL = 1024
DEPTH = 4
ATT_HEADS = 16
ATT_KV_HEADS = 4
ATT_GROUP = ATT_HEADS // ATT_KV_HEADS
ATT_HEAD_DIM = 64
WINDOW = 128
DN_HEADS = 8
DN_HEAD_DIM = 128
CONV_K = 4
CHUNK = 64
D_FF = 4 * D_MODEL
ATT_Q_W = ATT_HEADS * ATT_HEAD_DIM
ATT_KV_W = ATT_KV_HEADS * ATT_HEAD_DIM
DN_W = DN_HEADS * DN_HEAD_DIM
ALPHA = (2 * DEPTH) ** 0.25
LN_EPS = 1e-5
RMS_EPS = 1e-6

LANES = 128
SUBLANES = 8
CONV_HALO = 16
GATE_LANES = 2 * DN_HEADS
N_MAIN = ATT_Q_W + 4 * DN_W + 2 * D_MODEL + 2 * ATT_KV_W
COLBLK_Q, COLBLK_DQ, COLBLK_DK, COLBLK_DV, COLBLK_Z, COLBLK_GA, COLBLK_GB = range(7)
COLBLK_K = (ATT_Q_W + 4 * DN_W + 2 * D_MODEL) // ATT_KV_W
COLBLK_V = COLBLK_K + 1
VMEM_LIMIT = 48 * 1024 * 1024


def _params(*sem):
    return pltpu.CompilerParams(dimension_semantics=sem, vmem_limit_bytes=VMEM_LIMIT)


def _dot(a, b):
    return jnp.dot(a.astype(BF16), b.astype(BF16), preferred_element_type=F32)


def _dot_nt(a, b):
    return lax.dot_general(a.astype(BF16), b.astype(BF16), (((1,), (1,)), ((), ())), preferred_element_type=F32)


def _dot_tn(a, b):
    return lax.dot_general(a.astype(BF16), b.astype(BF16), (((0,), (0,)), ((), ())), preferred_element_type=F32)


def _dot_f32(a, b):
    return jnp.dot(a, b, precision=lax.Precision.HIGHEST, preferred_element_type=F32)


def _silu(t):
    half = 0.5 * t
    return half + half * jnp.tanh(half)


def _layer_norm(t, g, b):
    mu = jnp.mean(t, axis=-1, keepdims=True)
    tc = t - mu
    var = jnp.mean(tc * tc, axis=-1, keepdims=True)
    return tc * lax.rsqrt(var + LN_EPS) * g + b


def _ada_kernel(c_ref, w_ref, b_ref, o_ref):
    ca = _silu(c_ref[...])
    o_ref[...] = _dot(ca, w_ref[...]) + b_ref[...]


def _ada_mod(c, w_ada, b_ada):
    depth, d, n = w_ada.shape
    bsz = c.shape[0]
    tn = 1536
    return pl.pallas_call(
        _ada_kernel,
        grid=(depth, n // tn),
        in_specs=[
            pl.BlockSpec((bsz, d), lambda l, j: (0, 0)),
            pl.BlockSpec((None, d, tn), lambda l, j: (l, 0, j)),
            pl.BlockSpec((None, 1, tn), lambda l, j: (l, 0, j)),
        ],
        out_specs=pl.BlockSpec((None, bsz, tn), lambda l, j: (l, 0, j)),
        out_shape=jax.ShapeDtypeStruct((depth, bsz, n), F32),
        compiler_params=_params("parallel", "parallel"),
        name="ada_mod",
    )(c, w_ada, b_ada.reshape(depth, 1, n))


def _inproj_kernel(x_ref, mod_ref, w_ref, wg_ref, wgt_ref, proj_ref, gate_ref, gatet_ref, u_scr):
    j = pl.program_id(2)

    @pl.when(j == 0)
    def _():
        u = (x_ref[...] * (1.0 + mod_ref[1:2, :]) + mod_ref[0:1, :]).astype(BF16)
        u_scr[...] = u
        gate_ref[...] = jnp.dot(u, wg_ref[...], preferred_element_type=F32)
        for ci in range(u.shape[0] // CHUNK):
            uc = u[ci * CHUNK:(ci + 1) * CHUNK]
            gatet_ref[ci] = lax.dot_general(wgt_ref[...], uc, (((1,), (1,)), ((), ())), preferred_element_type=F32)

    proj_ref[...] = jnp.dot(u_scr[...], w_ref[...], preferred_element_type=F32).astype(BF16)


def _in_proj(x, mod_l, w_main, w_gate, w_gate_t):
    bsz, s, d = x.shape
    tm = min(1024, s)
    tn = 1536
    return pl.pallas_call(
        _inproj_kernel,
        grid=(bsz, s // tm, N_MAIN // tn),
        in_specs=[
            pl.BlockSpec((None, tm, d), lambda b, i, j: (b, i, 0)),
            pl.BlockSpec((None, 6, d), lambda b, i, j: (b, 0, 0)),
            pl.BlockSpec((d, tn), lambda b, i, j: (0, j)),
            pl.BlockSpec((d, LANES), lambda b, i, j: (0, 0)),
            pl.BlockSpec((GATE_LANES, d), lambda b, i, j: (0, 0)),
        ],
        out_specs=[
            pl.BlockSpec((None, tm, tn), lambda b, i, j: (b, i, j)),
            pl.BlockSpec((None, tm, LANES), lambda b, i, j: (b, i, 0)),
            pl.BlockSpec((None, tm // CHUNK, GATE_LANES, CHUNK), lambda b, i, j: (b, i, 0, 0)),
        ],
        out_shape=[
            jax.ShapeDtypeStruct((bsz, s, N_MAIN), BF16),
            jax.ShapeDtypeStruct((bsz, s, LANES), F32),
            jax.ShapeDtypeStruct((bsz, s // CHUNK, GATE_LANES, CHUNK), F32),
        ],
        scratch_shapes=[pltpu.VMEM((tm, d), BF16)],
        compiler_params=_params("parallel", "parallel", "arbitrary"),
        name="in_proj",
    )(x, mod_l, w_main, w_gate, w_gate_t)


def _attn_kernel(sink_ref, q_ref, kp_ref, kc_ref, vp_ref, vc_ref, o_ref, *, tq):
    i = pl.program_id(1)
    nsub = tq // WINDOW
    hd, grp, kvh = ATT_HEAD_DIM, ATT_GROUP, ATT_KV_HEADS
    assert 2 * hd == LANES and grp == 4
    kfull = jnp.concatenate([kp_ref[...], kc_ref[...]], axis=0).astype(F32)
    vfull = jnp.concatenate([vp_ref[...], vc_ref[...]], axis=0).astype(F32)
    lane = lax.broadcasted_iota(jnp.int32, (kfull.shape[0], LANES), 1)
    lo_mask = lane < hd

    def lo_hi(full, kv):
        blk = full[:, (kv // 2) * LANES:(kv // 2 + 1) * LANES]
        swapped = pltpu.roll(blk, hd, axis=1)
        src_lo, src_hi = (blk, swapped) if kv % 2 == 0 else (swapped, blk)
        return jnp.where(lo_mask, src_lo, 0.0).astype(BF16), jnp.where(lo_mask, 0.0, src_hi).astype(BF16)

    qi = lax.broadcasted_iota(jnp.int32, (WINDOW, 2 * WINDOW), 0)
    si = lax.broadcasted_iota(jnp.int32, (WINDOW, 2 * WINDOW), 1)
    diff = qi + WINDOW - si
    band = (diff >= 0) & (diff < WINDOW)
    band0 = band & ((si >= WINDOW) | (i > 0))
    neg_inf = jnp.float32(-jnp.inf)

    kv_lo_hi = [(lo_hi(kfull, kv), lo_hi(vfull, kv)) for kv in range(kvh)]
    units = [(kv, s) for kv in range(kvh) for s in range(nsub)]

    def scores(kv, s):
        c0 = kv * grp * hd
        rows = slice(s * WINDOW, (s + 1) * WINDOW)
        band_rows = slice(s * WINDOW, (s + 2) * WINDOW)
        (k_lo, k_hi), _ = kv_lo_hi[kv]
        q2 = jnp.concatenate([q_ref[rows, c0:c0 + LANES], q_ref[rows, c0 + LANES:c0 + 2 * LANES]], axis=0)
        sc_lo = lax.dot_general(q2, k_lo[band_rows], (((1,), (1,)), ((), ())), preferred_element_type=F32)
        sc_hi = lax.dot_general(q2, k_hi[band_rows], (((1,), (1,)), ((), ())), preferred_element_type=F32)
        return sc_lo[:WINDOW], sc_hi[:WINDOW], sc_lo[WINDOW:], sc_hi[WINDOW:]

    sc_next = scores(*units[0])
    for n, (kv, s) in enumerate(units):
        sc_cur = sc_next
        if n + 1 < len(units):
            sc_next = scores(*units[n + 1])
        c0 = kv * grp * hd
        rows = slice(s * WINDOW, (s + 1) * WINDOW)
        band_rows = slice(s * WINDOW, (s + 2) * WINDOW)
        _, (v_lo, v_hi) = kv_lo_hi[kv]
        valid = band0 if s == 0 else band
        probs = []
        for g, sc_g in enumerate(sc_cur):
            sg = jnp.where(valid, sc_g, neg_inf)
            sink = sink_ref[kv * grp + g]
            m = jnp.maximum(jnp.max(sg, axis=-1, keepdims=True), sink)
            p = jnp.exp(sg - m)
            denom = jnp.sum(p, axis=-1, keepdims=True) + jnp.exp(sink - m)
            probs.append((p / denom).astype(BF16))
        pcat = jnp.concatenate([jnp.concatenate(probs[0:2], axis=1), jnp.concatenate(probs[2:4], axis=1)], axis=0)
        vcat = jnp.concatenate([v_lo[band_rows], v_hi[band_rows]], axis=0)
        o = jnp.dot(pcat, vcat, preferred_element_type=F32)
        o_ref[rows, c0:c0 + LANES] = o[:WINDOW].astype(BF16)
        o_ref[rows, c0 + LANES:c0 + 2 * LANES] = o[WINDOW:].astype(BF16)


def _attention(proj, sinks):
    bsz, s, _ = proj.shape
    tq = min(512, s)
    sub = tq // WINDOW
    kern = functools.partial(_attn_kernel, tq=tq)
    return pl.pallas_call(
        kern,
        grid=(bsz, s // tq),
        in_specs=[
            pl.BlockSpec(memory_space=pltpu.SMEM),
            pl.BlockSpec((None, tq, ATT_Q_W), lambda b, i: (b, i, COLBLK_Q)),
            pl.BlockSpec((None, WINDOW, ATT_KV_W), lambda b, i: (b, jnp.maximum(i * sub - 1, 0), COLBLK_K)),
            pl.BlockSpec((None, tq, ATT_KV_W), lambda b, i: (b, i, COLBLK_K)),
            pl.BlockSpec((None, WINDOW, ATT_KV_W), lambda b, i: (b, jnp.maximum(i * sub - 1, 0), COLBLK_V)),
            pl.BlockSpec((None, tq, ATT_KV_W), lambda b, i: (b, i, COLBLK_V)),
        ],
        out_specs=pl.BlockSpec((None, tq, ATT_Q_W), lambda b, i: (b, i, 0)),
        out_shape=jax.ShapeDtypeStruct((bsz, s, ATT_Q_W), BF16),
        compiler_params=_params("parallel", "parallel"),
        name="swa_attention",
    )(sinks, proj, proj, proj, proj, proj)


def _gdn_kernel(dq_ref, dk_ref, dv_ref, z_ref, gate_ref, gatet_ref, convw_ref, acol_ref, dtcol_ref,
                arow_ref, dtrow_ref, nw_ref, o_ref, xpad, state, u_s, w_s, qdec_s, kdec_s, intra_s, last_s,
                hqdec_s, hkdec_s, hintra_s, hlast_s, hm_s, hrhs_s, *, ts):
    t = pl.program_id(1)
    nh, hd = DN_HEADS, DN_HEAD_DIM
    halo = CONV_HALO

    @pl.when(t == 0)
    def _():
        state[...] = jnp.zeros_like(state)
        for a in range(3):
            xpad[a, 0:halo, :] = jnp.zeros((halo, DN_W), BF16)

    @pl.when(t > 0)
    def _():
        for a in range(3):
            xpad[a, 0:halo, :] = xpad[a, ts:ts + halo, :]

    for a, ref in enumerate((dq_ref, dk_ref, dv_ref)):
        xpad[a, halo:halo + ts, :] = ref[...]

    sr = lax.broadcasted_iota(jnp.int32, ((CONV_K - 1) * CHUNK, CHUNK + halo), 0)
    sc = lax.broadcasted_iota(jnp.int32, ((CONV_K - 1) * CHUNK, CHUNK + halo), 1)
    shift_sel = (sc == (sr & (CHUNK - 1)) + halo - 1 - (sr >> 6)).astype(BF16)

    ri = lax.broadcasted_iota(jnp.int32, (CHUNK, CHUNK), 0)
    ci = lax.broadcasted_iota(jnp.int32, (CHUNK, CHUNK), 1)
    causal = ri >= ci
    strict = ri > ci
    tri_lo = causal.astype(F32)
    tri_up = (ri <= ci).astype(F32)
    eye = (ri == ci).astype(F32)
    neg_inf = jnp.float32(-jnp.inf)

    heads = range(nh)
    hsl = [slice(h * hd, (h + 1) * hd) for h in heads]

    def prep(c):
        sb = (c & 1) * nh
        r0 = pl.multiple_of(c * CHUNK, CHUNK)
        gcols = gate_ref[pl.ds(r0, CHUNK), :]
        beta_cols = jax.nn.sigmoid(gcols)
        g_cols = -jnp.exp(acol_ref[...]) * jax.nn.softplus(gcols + dtcol_ref[...])
        gc_cols = _dot_f32(tri_lo, g_cols)
        grows = gatet_ref[c]
        g_rows = -jnp.exp(arow_ref[...]) * jax.nn.softplus(grows + dtrow_ref[...])
        gc_rows = _dot_f32(g_rows, tri_up)
        yield

        qs, ks, vs = [], [], []
        for hp in range(nh // 2):
            if hp:
                yield
            lanes = slice(2 * hp * hd, (2 * hp + 2) * hd)
            xs = []
            for a in range(3):
                cs = slice(a * DN_W + 2 * hp * hd, a * DN_W + (2 * hp + 2) * hd)
                win = xpad[a, pl.ds(r0, CHUNK + halo), lanes]
                back = jnp.dot(shift_sel, win, preferred_element_type=F32)
                acc = win[halo:halo + CHUNK].astype(F32) * convw_ref[CONV_K - 1:CONV_K, cs]
                for jj in range(CONV_K - 1):
                    sh = CONV_K - 1 - jj
                    acc = acc + back[(sh - 1) * CHUNK:sh * CHUNK] * convw_ref[jj:jj + 1, cs]
                xs.append(_silu(acc))
            for hh in range(2):
                q, k, v = (x[:, hh * hd:(hh + 1) * hd] for x in xs)
                qs.append(q * lax.rsqrt(jnp.sum(q * q, axis=-1, keepdims=True) + RMS_EPS) * (hd ** -0.5))
                ks.append(k * lax.rsqrt(jnp.sum(k * k, axis=-1, keepdims=True) + RMS_EPS))
                vs.append(v)
        yield

        gcbs = [jnp.broadcast_to(gc_cols[:, nh + h:nh + h + 1], (CHUNK, hd)) for h in heads]
        gc_lasts = [gc_rows[nh + h:nh + h + 1, CHUNK - 1:CHUNK] for h in heads]
        decays = [jnp.exp(jnp.where(causal, gcbs[h][:, :CHUNK] - gc_rows[nh + h:nh + h + 1, :], neg_inf)) for h in heads]
        egcs = [jnp.exp(gcbs[h]) for h in heads]
        betas = [jnp.broadcast_to(beta_cols[:, h:h + 1], (CHUNK, hd)) for h in heads]
        kbs = [ks[h] * betas[h] for h in heads]
        vbs = [vs[h] * betas[h] for h in heads]
        yield

        kkqk = [_dot_nt(jnp.concatenate([kbs[h], qs[h]], axis=0), ks[h]) for h in heads]
        yield
        for h in heads:
            hintra_s[sb + h] = (kkqk[h][CHUNK:] * decays[h]).astype(BF16)
            hqdec_s[sb + h] = (qs[h] * egcs[h]).astype(BF16)
            hkdec_s[sb + h] = (ks[h] * jnp.exp(gc_lasts[h] - gcbs[h])).astype(BF16)
            hlast_s[sb + h] = jnp.broadcast_to(jnp.exp(gc_lasts[h]), (SUBLANES, hd))
            hm_s[sb + h] = -jnp.where(strict, kkqk[h][:CHUNK] * decays[h], 0.0)
            hrhs_s[sb + h] = jnp.concatenate([vbs[h], kbs[h] * egcs[h]], axis=1).astype(BF16)

    def solve(c):
        sb = (c & 1) * nh
        ms = [hm_s[sb + h] for h in heads]
        rhss = [hrhs_s[sb + h] for h in heads]
        for h in heads:
            intra_s[sb + h] = hintra_s[sb + h]
            qdec_s[sb + h] = hqdec_s[sb + h]
            kdec_s[sb + h] = hkdec_s[sb + h]
            last_s[sb + h] = hlast_s[sb + h]
        xinvs = [eye + ms[h] for h in heads]
        ms = [_dot(ms[h], ms[h]) for h in heads]
        yield
        npow = 2
        while npow < CHUNK:
            if npow * 2 < CHUNK:
                both = [_dot(jnp.concatenate([xinvs[h], ms[h]], axis=0), ms[h]) for h in heads]
                yield
                xinvs = [xinvs[h] + both[h][:CHUNK] for h in heads]
                ms = [both[h][CHUNK:] for h in heads]
            else:
                tails = [_dot(xinvs[h], ms[h]) for h in heads]
                yield
                xinvs = [xinvs[h] + tails[h] for h in heads]
            npow *= 2
        sols = [_dot(xinvs[h], rhss[h]) for h in heads]
        yield
        for h in heads:
            u_s[sb + h] = sols[h][:, :hd]
            w_s[sb + h] = sols[h][:, hd:].astype(BF16)

    def recur(c):
        r0 = pl.multiple_of(c * CHUNK, CHUNK)
        sb = (c & 1) * nh
        sts = [state[h] for h in heads]
        us = [u_s[sb + h] for h in heads]
        kdecs = [kdec_s[sb + h] for h in heads]
        intras = [intra_s[sb + h] for h in heads]
        lasts = [last_s[sb + h][0:1, :] for h in heads]
        ws_qs = [_dot(jnp.concatenate([w_s[sb + h], qdec_s[sb + h]], axis=0), sts[h]) for h in heads]
        yield
        v_news = [us[h] - ws_qs[h][:CHUNK] for h in heads]
        upds = [_dot_tn(kdecs[h], v_news[h]) for h in heads]
        outs = [ws_qs[h][CHUNK:] + _dot(intras[h], v_news[h]) for h in heads]
        yield
        for h in heads:
            state[h] = sts[h] * lasts[h] + upds[h]
        for h in heads:
            o = outs[h]
            o = o * lax.rsqrt(jnp.mean(o * o, axis=-1, keepdims=True) + RMS_EPS) * nw_ref[...]
            o = o * _silu(z_ref[pl.ds(r0, CHUNK), hsl[h]].astype(F32))
            o_ref[pl.ds(r0, CHUNK), hsl[h]] = o.astype(BF16)

    def run_interleaved(*gens):
        live = list(gens)
        while live:
            for g in list(live):
                try:
                    next(g)
                except StopIteration:
                    live.remove(g)

    nchunk = ts // CHUNK
    assert nchunk >= 2
    run_interleaved(prep(0))
    run_interleaved(solve(0), prep(1))

    def chunk_body(c, carry):
        run_interleaved(recur(c), solve(c + 1), prep(c + 2))
        return carry

    lax.fori_loop(0, nchunk - 2, chunk_body, 0)
    run_interleaved(recur(nchunk - 2), solve(nchunk - 1))
    run_interleaved(recur(nchunk - 1))


def _gated_deltanet(proj, gate, gate_t, conv_w, a_log, dt_bias, norm_w):
    bsz, s, _ = proj.shape
    ts = min(1024, s)
    nh = DN_HEADS
    zpad = jnp.zeros((nh,), F32)
    acol = jnp.concatenate([zpad, a_log, jnp.zeros((LANES - 2 * nh,), F32)]).reshape(1, LANES)
    dtcol = jnp.concatenate([zpad, dt_bias, jnp.zeros((LANES - 2 * nh,), F32)]).reshape(1, LANES)
    arow = jnp.broadcast_to(jnp.concatenate([zpad, a_log]).reshape(GATE_LANES, 1), (GATE_LANES, CHUNK))
    dtrow = jnp.broadcast_to(jnp.concatenate([zpad, dt_bias]).reshape(GATE_LANES, 1), (GATE_LANES, CHUNK))
    kern = functools.partial(_gdn_kernel, ts=ts)
    const = lambda b, t: (0, 0)
    return pl.pallas_call(
        kern,
        grid=(bsz, s // ts),
        in_specs=[
            pl.BlockSpec((None, ts, DN_W), lambda b, t: (b, t, COLBLK_DQ)),
            pl.BlockSpec((None, ts, DN_W), lambda b, t: (b, t, COLBLK_DK)),
            pl.BlockSpec((None, ts, DN_W), lambda b, t: (b, t, COLBLK_DV)),
            pl.BlockSpec((None, ts, DN_W), lambda b, t: (b, t, COLBLK_Z)),
            pl.BlockSpec((None, ts, LANES), lambda b, t: (b, t, 0)),
            pl.BlockSpec((None, ts // CHUNK, GATE_LANES, CHUNK), lambda b, t: (b, t, 0, 0)),
            pl.BlockSpec((CONV_K, 3 * DN_W), const),
            pl.BlockSpec((1, LANES), const),
            pl.BlockSpec((1, LANES), const),
            pl.BlockSpec((GATE_LANES, CHUNK), const),
            pl.BlockSpec((GATE_LANES, CHUNK), const),
            pl.BlockSpec((1, DN_HEAD_DIM), const),
        ],
        out_specs=pl.BlockSpec((None, ts, DN_W), lambda b, t: (b, t, 0)),
        out_shape=jax.ShapeDtypeStruct((bsz, s, DN_W), BF16),
        scratch_shapes=[
            pltpu.VMEM((3, ts + CONV_HALO, DN_W), BF16),
            pltpu.VMEM((DN_HEADS, DN_HEAD_DIM, DN_HEAD_DIM), F32),
            pltpu.VMEM((2 * DN_HEADS, CHUNK, DN_HEAD_DIM), F32),
            pltpu.VMEM((2 * DN_HEADS, CHUNK, DN_HEAD_DIM), BF16),
            pltpu.VMEM((2 * DN_HEADS, CHUNK, DN_HEAD_DIM), BF16),
            pltpu.VMEM((2 * DN_HEADS, CHUNK, DN_HEAD_DIM), BF16),
            pltpu.VMEM((2 * DN_HEADS, CHUNK, CHUNK), BF16),
            pltpu.VMEM((2 * DN_HEADS, SUBLANES, DN_HEAD_DIM), F32),
            pltpu.VMEM((2 * DN_HEADS, CHUNK, DN_HEAD_DIM), BF16),
            pltpu.VMEM((2 * DN_HEADS, CHUNK, DN_HEAD_DIM), BF16),
            pltpu.VMEM((2 * DN_HEADS, CHUNK, CHUNK), BF16),
            pltpu.VMEM((2 * DN_HEADS, SUBLANES, DN_HEAD_DIM), F32),
            pltpu.VMEM((2 * DN_HEADS, CHUNK, CHUNK), F32),
            pltpu.VMEM((2 * DN_HEADS, CHUNK, 2 * DN_HEAD_DIM), BF16),
        ],
        compiler_params=_params("parallel", "arbitrary"),
        name="gated_deltanet",
    )(proj, proj, proj, proj, gate, gate_t, conv_w, acol, dtcol, arow, dtrow, norm_w.reshape(1, DN_HEAD_DIM))


def _mix_kernel(att_ref, gdn_ref, ga_ref, gb_ref, x_ref, mod_ref, woa_ref, wob_ref, wout_ref, g_ref, b_ref, o_ref):
    y_a = jnp.dot(att_ref[...], woa_ref[...], preferred_element_type=F32)
    y_b = jnp.dot(gdn_ref[...], wob_ref[...], preferred_element_type=F32)
    merged = jax.nn.sigmoid(ga_ref[...].astype(F32)) * y_a + jax.nn.sigmoid(gb_ref[...].astype(F32)) * y_b
    mixed = jnp.dot(merged.astype(BF16), wout_ref[...], preferred_element_type=F32)
    o_ref[...] = _layer_norm(ALPHA * x_ref[...] + (1.0 + mod_ref[2:3, :]) * mixed, g_ref[...], b_ref[...])


def _mix(att, gdn, proj, x, mod_l, w_oa, w_ob, w_out, ln_g, ln_b):
    bsz, s, d = x.shape
    tm = min(512, s)
    row = lambda b, i: (b, i, 0)
    const = lambda b, i: (0, 0)
    return pl.pallas_call(
        _mix_kernel,
        grid=(bsz, s // tm),
        in_specs=[
            pl.BlockSpec((None, tm, d), row),
            pl.BlockSpec((None, tm, d), row),
            pl.BlockSpec((None, tm, d), lambda b, i: (b, i, COLBLK_GA)),
            pl.BlockSpec((None, tm, d), lambda b, i: (b, i, COLBLK_GB)),
            pl.BlockSpec((None, tm, d), row),
            pl.BlockSpec((None, 6, d), lambda b, i: (b, 0, 0)),
            pl.BlockSpec((d, d), const),
            pl.BlockSpec((d, d), const),
            pl.BlockSpec((d, d), const),
            pl.BlockSpec((1, d), const),
            pl.BlockSpec((1, d), const),
        ],
        out_specs=pl.BlockSpec((None, tm, d), row),
        out_shape=jax.ShapeDtypeStruct((bsz, s, d), F32),
        compiler_params=_params("parallel", "parallel"),
        name="mix_out_ln",
    )(att, gdn, proj, proj, x, mod_l, w_oa, w_ob, w_out, ln_g.reshape(1, d), ln_b.reshape(1, d))


def _ffn_kernel(x_ref, mod_ref, w1_ref, b1_ref, w2_ref, b2_ref, g_ref, b_ref, o_ref, u_scr, acc):
    f = pl.program_id(2)

    @pl.when(f == 0)
    def _():
        u_scr[...] = (x_ref[...] * (1.0 + mod_ref[4:5, :]) + mod_ref[3:4, :]).astype(BF16)
        acc[...] = jnp.zeros_like(acc)

    h = jnp.dot(u_scr[...], w1_ref[...], preferred_element_type=F32) + b1_ref[...]
    h = jnp.square(jnp.maximum(h, 0.0))
    acc[...] += jnp.dot(h.astype(BF16), w2_ref[...], preferred_element_type=F32)

    @pl.when(f == pl.num_programs(2) - 1)
    def _():
        y = acc[...] + b2_ref[...]
        o_ref[...] = _layer_norm(ALPHA * x_ref[...] + (1.0 + mod_ref[5:6, :]) * y, g_ref[...], b_ref[...])


def _ffn(x, mod_l, w1, b1, w2, b2, ln_g, ln_b):
    bsz, s, d = x.shape
    dff = w1.shape[1]
    tm = min(1024, s)
    tf = 1024
    row = lambda b, i, f: (b, i, 0)
    const = lambda b, i, f: (0, 0)
    return pl.pallas_call(
        _ffn_kernel,
        grid=(bsz, s // tm, dff // tf),
        in_specs=[
            pl.BlockSpec((None, tm, d), row),
            pl.BlockSpec((None, 6, d), lambda b, i, f: (b, 0, 0)),
            pl.BlockSpec((d, tf), lambda b, i, f: (0, f)),
            pl.BlockSpec((1, tf), lambda b, i, f: (0, f)),
            pl.BlockSpec((tf, d), lambda b, i, f: (f, 0)),
            pl.BlockSpec((1, d), const),
            pl.BlockSpec((1, d), const),
            pl.BlockSpec((1, d), const),
        ],
        out_specs=pl.BlockSpec((None, tm, d), row),
        out_shape=jax.ShapeDtypeStruct((bsz, s, d), F32),
        scratch_shapes=[pltpu.VMEM((tm, d), BF16), pltpu.VMEM((tm, d), F32)],
        compiler_params=_params("parallel", "parallel", "arbitrary"),
        name="ffn_ln",
    )(x, mod_l, w1, b1.reshape(1, dff), w2, b2.reshape(1, d), ln_g.reshape(1, d), ln_b.reshape(1, d))


def kernel(x, c, w_ada, b_ada, w_in, conv_w, a_log, dt_bias, sinks, dn_norm_w, w_oa, w_ob, w_out,
           ln1_g, ln1_b, w_ff1, b_ff1, w_ff2, b_ff2, ln2_g, ln2_b):
    depth = w_ada.shape[0]
    bsz, s, d = x.shape
    mod = _ada_mod(c, w_ada, b_ada).reshape(depth, bsz, 6, d)
    gate_lo = ATT_Q_W + 2 * ATT_KV_W + 4 * DN_W
    gate_hi = gate_lo + GATE_LANES
    w_gate_f32 = w_in[:, :, gate_lo:gate_hi]
    w_gate_all = jnp.pad(w_gate_f32, ((0, 0), (0, 0), (0, LANES - GATE_LANES))).astype(BF16)
    w_gate_t_all = jnp.swapaxes(w_gate_f32, 1, 2).astype(BF16)
    for l in range(depth):
        wl = w_in[l]
        w_main = jnp.concatenate(
            [wl[:, :ATT_Q_W] * (ATT_HEAD_DIM ** -0.5), wl[:, ATT_Q_W + 2 * ATT_KV_W:gate_lo], wl[:, gate_hi:],
             wl[:, ATT_Q_W:ATT_Q_W + 2 * ATT_KV_W]], axis=1).astype(BF16)
        proj, gate, gate_t = _in_proj(x, mod[l], w_main, w_gate_all[l], w_gate_t_all[l])
        att = _attention(proj, sinks[l])
        gdn = _gated_deltanet(proj, gate, gate_t, conv_w[l], a_log[l], dt_bias[l], dn_norm_w[l])
        x = _mix(att, gdn, proj, x, mod[l], w_oa[l].astype(BF16), w_ob[l].astype(BF16), w_out[l].astype(BF16),
                 ln1_g[l], ln1_b[l])
        x = _ffn(x, mod[l], w_ff1[l].astype(BF16), b_ff1[l], w_ff2[l].astype(BF16), b_ff2[l], ln2_g[l], ln2_b[l])
    return x
```

```python
import functools

import jax
import jax.numpy as jnp
from jax import lax
from jax.experimental import pallas as pl
from jax.experimental.pallas import tpu as pltpu

F32 = jnp.float32
BF16 = jnp.bfloat16

D_MODEL = 1024
DEPTH = 4
ATT_HEADS = 16
ATT_KV_HEADS = 4
ATT_GROUP = ATT_HEADS // ATT_KV_HEADS
ATT_HEAD_DIM = 64
WINDOW = 128
DN_HEADS = 8
DN_HEAD_DIM = 128
CONV_K = 4
CHUNK = 64
D_FF = 4 * D_MODEL
ATT_Q_W = ATT_HEADS * ATT_HEAD_DIM
ATT_KV_W = ATT_KV_HEADS * ATT_HEAD_DIM
DN_W = DN_HEADS * DN_HEAD_DIM
ALPHA = (2 * DEPTH) ** 0.25
LN_EPS = 1e-5
RMS_EPS = 1e-6

LANES = 128
SUBLANES = 8
CONV_HALO = 16
GATE_LANES = 2 * DN_HEADS
N_MAIN = ATT_Q_W + 4 * DN_W + 2 * D_MODEL + 2 * ATT_KV_W
COLBLK_Q, COLBLK_DQ, COLBLK_DK, COLBLK_DV, COLBLK_Z, COLBLK_GA, COLBLK_GB = range(7)
COLBLK_K = (ATT_Q_W + 4 * DN_W + 2 * D_MODEL) // ATT_KV_W
COLBLK_V = COLBLK_K + 1
VMEM_LIMIT = 48 * 1024 * 1024


def _params(*sem):
    return pltpu.CompilerParams(dimension_semantics=sem, vmem_limit_bytes=VMEM_LIMIT)


def _dot(a, b):
    return jnp.dot(a.astype(BF16), b.astype(BF16), preferred_element_type=F32)


def _dot_nt(a, b):
    return lax.dot_general(a.astype(BF16), b.astype(BF16), (((1,), (1,)), ((), ())), preferred_element_type=F32)


def _dot_tn(a, b):
    return lax.dot_general(a.astype(BF16), b.astype(BF16), (((0,), (0,)), ((), ())), preferred_element_type=F32)


def _dot_f32(a, b):
    return jnp.dot(a, b, precision=lax.Precision.HIGHEST, preferred_element_type=F32)


def _silu(t):
    half = 0.5 * t
    return half + half * jnp.tanh(half)


def _layer_norm(t, g, b):
    mu = jnp.mean(t, axis=-1, keepdims=True)
    tc = t - mu
    var = jnp.mean(tc * tc, axis=-1, keepdims=True)
    return tc * lax.rsqrt(var + LN_EPS) * g + b


def _ada_kernel(c_ref, w_ref, b_ref, o_ref):
    ca = _silu(c_ref[...])
    o_ref[...] = _dot(ca, w_ref[...]) + b_ref[...]


def _ada_mod(c, w_ada, b_ada):
    depth, d, n = w_ada.shape
    bsz = c.shape[0]
    tn = 1536
    return pl.pallas_call(
        _ada_kernel,
        grid=(depth, n // tn),
        in_specs=[
            pl.BlockSpec((bsz, d), lambda l, j: (0, 0)),
            pl.BlockSpec((None, d, tn), lambda l, j: (l, 0, j)),
            pl.BlockSpec((None, 1, tn), lambda l, j: (l, 0, j)),
        ],
        out_specs=pl.BlockSpec((None, bsz, tn), lambda l, j: (l, 0, j)),
        out_shape=jax.ShapeDtypeStruct((depth, bsz, n), F32),
        compiler_params=_params("parallel", "parallel"),
        name="ada_mod",
    )(c, w_ada, b_ada.reshape(depth, 1, n))


def _inproj_kernel(x_ref, mod_ref, w_ref, wg_ref, wgt_ref, proj_ref, gate_ref, gatet_ref, u_scr):
    j = pl.program_id(2)

    @pl.when(j == 0)
    def _():
        u = (x_ref[...] * (1.0 + mod_ref[1:2, :]) + mod_ref[0:1, :]).astype(BF16)
        u_scr[...] = u
        gate_ref[...] = jnp.dot(u, wg_ref[...], preferred_element_type=F32)
        for ci in range(u.shape[0] // CHUNK):
            uc = u[ci * CHUNK:(ci + 1) * CHUNK]
            gatet_ref[ci] = lax.dot_general(wgt_ref[...], uc, (((1,), (1,)), ((), ())), preferred_element_type=F32)

    proj_ref[...] = jnp.dot(u_scr[...], w_ref[...], preferred_element_type=F32).astype(BF16)


def _in_proj(x, mod_l, w_main, w_gate, w_gate_t):
    bsz, s, d = x.shape
    tm = min(1024, s)
    tn = 1536
    return pl.pallas_call(
        _inproj_kernel,
        grid=(bsz, s // tm, N_MAIN // tn),
        in_specs=[
            pl.BlockSpec((None, tm, d), lambda b, i, j: (b, i, 0)),
            pl.BlockSpec((None, 6, d), lambda b, i, j: (b, 0, 0)),
            pl.BlockSpec((d, tn), lambda b, i, j: (0, j)),
            pl.BlockSpec((d, LANES), lambda b, i, j: (0, 0)),
            pl.BlockSpec((GATE_LANES, d), lambda b, i, j: (0, 0)),
        ],
        out_specs=[
            pl.BlockSpec((None, tm, tn), lambda b, i, j: (b, i, j)),
            pl.BlockSpec((None, tm, LANES), lambda b, i, j: (b, i, 0)),
            pl.BlockSpec((None, tm // CHUNK, GATE_LANES, CHUNK), lambda b, i, j: (b, i, 0, 0)),
        ],
        out_shape=[
            jax.ShapeDtypeStruct((bsz, s, N_MAIN), BF16),
            jax.ShapeDtypeStruct((bsz, s, LANES), F32),
            jax.ShapeDtypeStruct((bsz, s // CHUNK, GATE_LANES, CHUNK), F32),
        ],
        scratch_shapes=[pltpu.VMEM((tm, d), BF16)],
        compiler_params=_params("parallel", "parallel", "arbitrary"),
        name="in_proj",
    )(x, mod_l, w_main, w_gate, w_gate_t)


def _attn_kernel(sink_ref, q_ref, kp_ref, kc_ref, vp_ref, vc_ref, o_ref, *, tq):
    i = pl.program_id(1)
    nsub = tq // WINDOW
    hd, grp, kvh = ATT_HEAD_DIM, ATT_GROUP, ATT_KV_HEADS
    assert 2 * hd == LANES and grp == 4
    kfull = jnp.concatenate([kp_ref[...], kc_ref[...]], axis=0).astype(F32)
    vfull = jnp.concatenate([vp_ref[...], vc_ref[...]], axis=0).astype(F32)
    lane = lax.broadcasted_iota(jnp.int32, (kfull.shape[0], LANES), 1)
    lo_mask = lane < hd

    def lo_hi(full, kv):
        blk = full[:, (kv // 2) * LANES:(kv // 2 + 1) * LANES]
        swapped = pltpu.roll(blk, hd, axis=1)
        src_lo, src_hi = (blk, swapped) if kv % 2 == 0 else (swapped, blk)
        return jnp.where(lo_mask, src_lo, 0.0).astype(BF16), jnp.where(lo_mask, 0.0, src_hi).astype(BF16)

    qi = lax.broadcasted_iota(jnp.int32, (WINDOW, 2 * WINDOW), 0)
    si = lax.broadcasted_iota(jnp.int32, (WINDOW, 2 * WINDOW), 1)
    diff = qi + WINDOW - si
    band = (diff >= 0) & (diff < WINDOW)
    band0 = band & ((si >= WINDOW) | (i > 0))
    neg_inf = jnp.float32(-jnp.inf)

    kv_lo_hi = [(lo_hi(kfull, kv), lo_hi(vfull, kv)) for kv in range(kvh)]
    units = [(kv, s) for kv in range(kvh) for s in range(nsub)]

    def scores(kv, s):
        c0 = kv * grp * hd
        rows = slice(s * WINDOW, (s + 1) * WINDOW)
        band_rows = slice(s * WINDOW, (s + 2) * WINDOW)
        (k_lo, k_hi), _ = kv_lo_hi[kv]
        q2 = jnp.concatenate([q_ref[rows, c0:c0 + LANES], q_ref[rows, c0 + LANES:c0 + 2 * LANES]], axis=0)
        sc_lo = lax.dot_general(q2, k_lo[band_rows], (((1,), (1,)), ((), ())), preferred_element_type=F32)
        sc_hi = lax.dot_general(q2, k_hi[band_rows], (((1,), (1,)), ((), ())), preferred_element_type=F32)
        return sc_lo[:WINDOW], sc_hi[:WINDOW], sc_lo[WINDOW:], sc_hi[WINDOW:]

    sc_next = scores(*units[0])
    for n, (kv, s) in enumerate(units):
        sc_cur = sc_next
        if n + 1 < len(units):
            sc_next = scores(*units[n + 1])
        c0 = kv * grp * hd
        rows = slice(s * WINDOW, (s + 1) * WINDOW)
        band_rows = slice(s * WINDOW, (s + 2) * WINDOW)
        _, (v_lo, v_hi) = kv_lo_hi[kv]
        valid = band0 if s == 0 else band
        probs = []
        for g, sc_g in enumerate(sc_cur):
            sg = jnp.where(valid, sc_g, neg_inf)
            sink = sink_ref[kv * grp + g]
            m = jnp.maximum(jnp.max(sg, axis=-1, keepdims=True), sink)
            p = jnp.exp(sg - m)
            denom = jnp.sum(p, axis=-1, keepdims=True) + jnp.exp(sink - m)
            probs.append((p / denom).astype(BF16))
        pcat = jnp.concatenate([jnp.concatenate(probs[0:2], axis=1), jnp.concatenate(probs[2:4], axis=1)], axis=0)
        vcat = jnp.concatenate([v_lo[band_rows], v_hi[band_rows]], axis=0)
        o = jnp.dot(pcat, vcat, preferred_element_type=F32)
        o_ref[rows, c0:c0 + LANES] = o[:WINDOW].astype(BF16)
        o_ref[rows, c0 + LANES:c0 + 2 * LANES] = o[WINDOW:].astype(BF16)


def _attention(proj, sinks):
    bsz, s, _ = proj.shape
    tq = min(512, s)
    sub = tq // WINDOW
    kern = functools.partial(_attn_kernel, tq=tq)
    return pl.pallas_call(
        kern,
        grid=(bsz, s // tq),
        in_specs=[
            pl.BlockSpec(memory_space=pltpu.SMEM),
            pl.BlockSpec((None, tq, ATT_Q_W), lambda b, i: (b, i, COLBLK_Q)),
            pl.BlockSpec((None, WINDOW, ATT_KV_W), lambda b, i: (b, jnp.maximum(i * sub - 1, 0), COLBLK_K)),
            pl.BlockSpec((None, tq, ATT_KV_W), lambda b, i: (b, i, COLBLK_K)),
            pl.BlockSpec((None, WINDOW, ATT_KV_W), lambda b, i: (b, jnp.maximum(i * sub - 1, 0), COLBLK_V)),
            pl.BlockSpec((None, tq, ATT_KV_W), lambda b, i: (b, i, COLBLK_V)),
        ],
        out_specs=pl.BlockSpec((None, tq, ATT_Q_W), lambda b, i: (b, i, 0)),
        out_shape=jax.ShapeDtypeStruct((bsz, s, ATT_Q_W), BF16),
        compiler_params=_params("parallel", "parallel"),
        name="swa_attention",
    )(sinks, proj, proj, proj, proj, proj)


def _gdn_kernel(dq_ref, dk_ref, dv_ref, z_ref, gate_ref, gatet_ref, convw_ref, acol_ref, dtcol_ref,
                arow_ref, dtrow_ref, nw_ref, o_ref, xpad, tail_s, state, u_s, w_s,
                hqdec_s, hkdec_s, hintra_s, hlast_s, hm_s, hrhs_s, *, ts):
    t = pl.program_id(1)
    nh, hd = DN_HEADS, DN_HEAD_DIM
    halo = CONV_HALO

    @pl.when(t == 0)
    def _():
        state[...] = jnp.zeros_like(state)
        for a in range(3):
            xpad[a, 0:halo, :] = jnp.zeros((halo, DN_W), BF16)

    @pl.when(t > 0)
    def _():
        for a in range(3):
            xpad[a, 0:halo, :] = tail_s[a]

    for a, ref in enumerate((dq_ref, dk_ref, dv_ref)):
        xpad[a, halo:halo + ts, :] = ref[...]

    sr = lax.broadcasted_iota(jnp.int32, ((CONV_K - 1) * CHUNK, CHUNK + halo), 0)
    sc = lax.broadcasted_iota(jnp.int32, ((CONV_K - 1) * CHUNK, CHUNK + halo), 1)
    shift_sel = (sc == (sr & (CHUNK - 1)) + halo - 1 - (sr >> 6)).astype(BF16)

    ri = lax.broadcasted_iota(jnp.int32, (CHUNK, CHUNK), 0)
    ci = lax.broadcasted_iota(jnp.int32, (CHUNK, CHUNK), 1)
    causal = ri >= ci
    strict = ri > ci
    tri_lo = causal.astype(F32)
    tri_up = (ri <= ci).astype(F32)
    eye = (ri == ci).astype(F32)
    neg_inf = jnp.float32(-jnp.inf)

    heads = range(nh)
    hsl = [slice(h * hd, (h + 1) * hd) for h in heads]

    def slot3(c):
        return (c % 3 if isinstance(c, int) else lax.rem(c, 3)) * nh

    def prep(c):
        sb = (c & 1) * nh
        sb3 = slot3(c)
        r0 = pl.multiple_of(c * CHUNK, CHUNK)
        gcols = gate_ref[pl.ds(r0, CHUNK), :]
        beta_cols = jax.nn.sigmoid(gcols)
        g_cols = -jnp.exp(acol_ref[...]) * jax.nn.softplus(gcols + dtcol_ref[...])
        gc_cols = _dot_f32(tri_lo, g_cols)
        grows = gatet_ref[c]
        g_rows = -jnp.exp(arow_ref[...]) * jax.nn.softplus(grows + dtrow_ref[...])
        gc_rows = _dot_f32(g_rows, tri_up)
        yield

        qs, ks, vs = [], [], []
        for hp in range(nh // 2):
            if hp:
                yield
            lanes = slice(2 * hp * hd, (2 * hp + 2) * hd)
            xs = []
            for a in range(3):
                cs = slice(a * DN_W + 2 * hp * hd, a * DN_W + (2 * hp + 2) * hd)
                win = xpad[a, pl.ds(r0, CHUNK + halo), lanes]
                back = jnp.dot(shift_sel, win, preferred_element_type=F32)
                acc = win[halo:halo + CHUNK].astype(F32) * convw_ref[CONV_K - 1:CONV_K, cs]
                for jj in range(CONV_K - 1):
                    sh = CONV_K - 1 - jj
                    acc = acc + back[(sh - 1) * CHUNK:sh * CHUNK] * convw_ref[jj:jj + 1, cs]
                xs.append(_silu(acc))
            for hh in range(2):
                q, k, v = (x[:, hh * hd:(hh + 1) * hd] for x in xs)
                qs.append(q * lax.rsqrt(jnp.sum(q * q, axis=-1, keepdims=True) + RMS_EPS) * (hd ** -0.5))
                ks.append(k * lax.rsqrt(jnp.sum(k * k, axis=-1, keepdims=True) + RMS_EPS))
                vs.append(v)
        yield

        gcbs = [jnp.broadcast_to(gc_cols[:, nh + h:nh + h + 1], (CHUNK, hd)) for h in heads]
        gc_lasts = [gc_rows[nh + h:nh + h + 1, CHUNK - 1:CHUNK] for h in heads]
        decays = [jnp.exp(jnp.where(causal, gcbs[h][:, :CHUNK] - gc_rows[nh + h:nh + h + 1, :], neg_inf)) for h in heads]
        egcs = [jnp.exp(gcbs[h]) for h in heads]
        betas = [jnp.broadcast_to(beta_cols[:, h:h + 1], (CHUNK, hd)) for h in heads]
        kbs = [ks[h] * betas[h] for h in heads]
        vbs = [vs[h] * betas[h] for h in heads]
        for h in heads:
            hqdec_s[sb3 + h] = (qs[h] * egcs[h]).astype(BF16)
            hkdec_s[sb3 + h] = (ks[h] * jnp.exp(gc_lasts[h] - gcbs[h])).astype(BF16)
            hlast_s[sb3 + h] = jnp.broadcast_to(jnp.exp(gc_lasts[h]), (SUBLANES, hd))
            hrhs_s[sb + h] = jnp.concatenate([vbs[h], kbs[h] * egcs[h]], axis=1).astype(BF16)
        kbq = [jnp.concatenate([kbs[h], qs[h]], axis=0).astype(BF16) for h in heads]
        kbf = [ks[h].astype(BF16) for h in heads]
        yield

        kkqk = [_dot_nt(kbq[h], kbf[h]) for h in heads]
        yield
        for h in heads:
            hintra_s[sb3 + h] = (kkqk[h][CHUNK:] * decays[h]).astype(BF16)
            hm_s[sb + h] = -jnp.where(strict, kkqk[h][:CHUNK] * decays[h], 0.0)

    def solve(c):
        sb = (c & 1) * nh
        ms = [hm_s[sb + h] for h in heads]
        rhss = [hrhs_s[sb + h] for h in heads]
        xinvs = [eye + ms[h] for h in heads]
        ms = [_dot(ms[h], ms[h]) for h in heads]
        yield
        npow = 2
        while npow < CHUNK:
            if npow * 2 < CHUNK:
                both = [_dot(jnp.concatenate([xinvs[h], ms[h]], axis=0), ms[h]) for h in heads]
                yield
                xinvs = [xinvs[h] + both[h][:CHUNK] for h in heads]
                ms = [both[h][CHUNK:] for h in heads]
            else:
                tails = [_dot(xinvs[h], ms[h]) for h in heads]
                yield
                xinvs = [xinvs[h] + tails[h] for h in heads]
            npow *= 2
        sols = [_dot(xinvs[h], rhss[h]) for h in heads]
        yield
        for h in heads:
            u_s[sb + h] = sols[h][:, :hd]
            w_s[sb + h] = sols[h][:, hd:].astype(BF16)

    def recur(c):
        r0 = pl.multiple_of(c * CHUNK, CHUNK)
        sb = (c & 1) * nh
        sb3 = slot3(c)
        sts = [state[h] for h in heads]
        us = [u_s[sb + h] for h in heads]
        kdecs = [hkdec_s[sb3 + h] for h in heads]
        intras = [hintra_s[sb3 + h] for h in heads]
        lasts = [hlast_s[sb3 + h][0:1, :] for h in heads]
        ws_qs = [_dot(jnp.concatenate([w_s[sb + h], hqdec_s[sb3 + h]], axis=0), sts[h]) for h in heads]
        yield
        v_news = [us[h] - ws_qs[h][:CHUNK] for h in heads]
        upds = [_dot_tn(kdecs[h], v_news[h]) for h in heads]
        outs = [ws_qs[h][CHUNK:] + _dot(intras[h], v_news[h]) for h in heads]
        yield
        for h in heads:
            state[h] = sts[h] * lasts[h] + upds[h]
        for h in heads:
            o = outs[h]
            o = o * lax.rsqrt(jnp.mean(o * o, axis=-1, keepdims=True) + RMS_EPS) * nw_ref[...]
            o = o * _silu(z_ref[pl.ds(r0, CHUNK), hsl[h]].astype(F32))
            o_ref[pl.ds(r0, CHUNK), hsl[h]] = o.astype(BF16)

    def run_interleaved(*gens):
        live = list(gens)
        while live:
            for g in list(live):
                try:
                    next(g)
                except StopIteration:
                    live.remove(g)

    nchunk = ts // CHUNK
    assert nchunk >= 2
    run_interleaved(prep(0))
    run_interleaved(solve(0), prep(1))

    def chunk_body(c, carry):
        run_interleaved(recur(c), solve(c + 1), prep(c + 2))
        return carry

    lax.fori_loop(0, nchunk - 2, chunk_body, 0)
    run_interleaved(recur(nchunk - 2), solve(nchunk - 1))
    run_interleaved(recur(nchunk - 1))
    for a in range(3):
        tail_s[a] = xpad[a, ts:ts + halo, :]


def _gated_deltanet(proj, gate, gate_t, conv_w, a_log, dt_bias, norm_w):
    bsz, s, _ = proj.shape
    ts = min(1024, s)
    nh = DN_HEADS
    zpad = jnp.zeros((nh,), F32)
    acol = jnp.concatenate([zpad, a_log, jnp.zeros((LANES - 2 * nh,), F32)]).reshape(1, LANES)
    dtcol = jnp.concatenate([zpad, dt_bias, jnp.zeros((LANES - 2 * nh,), F32)]).reshape(1, LANES)
    arow = jnp.broadcast_to(jnp.concatenate([zpad, a_log]).reshape(GATE_LANES, 1), (GATE_LANES, CHUNK))
    dtrow = jnp.broadcast_to(jnp.concatenate([zpad, dt_bias]).reshape(GATE_LANES, 1), (GATE_LANES, CHUNK))
    kern = functools.partial(_gdn_kernel, ts=ts)
    const = lambda b, t: (0, 0)
    return pl.pallas_call(
        kern,
        grid=(bsz, s // ts),
        in_specs=[
            pl.BlockSpec((None, ts, DN_W), lambda b, t: (b, t, COLBLK_DQ)),
            pl.BlockSpec((None, ts, DN_W), lambda b, t: (b, t, COLBLK_DK)),
            pl.BlockSpec((None, ts, DN_W), lambda b, t: (b, t, COLBLK_DV)),
            pl.BlockSpec((None, ts, DN_W), lambda b, t: (b, t, COLBLK_Z)),
            pl.BlockSpec((None, ts, LANES), lambda b, t: (b, t, 0)),
            pl.BlockSpec((None, ts // CHUNK, GATE_LANES, CHUNK), lambda b, t: (b, t, 0, 0)),
            pl.BlockSpec((CONV_K, 3 * DN_W), const),
            pl.BlockSpec((1, LANES), const),
            pl.BlockSpec((1, LANES), const),
            pl.BlockSpec((GATE_LANES, CHUNK), const),
            pl.BlockSpec((GATE_LANES, CHUNK), const),
            pl.BlockSpec((1, DN_HEAD_DIM), const),
        ],
        out_specs=pl.BlockSpec((None, ts, DN_W), lambda b, t: (b, t, 0)),
        out_shape=jax.ShapeDtypeStruct((bsz, s, DN_W), BF16),
        scratch_shapes=[
            pltpu.VMEM((3, ts + CONV_HALO, DN_W), BF16),
            pltpu.VMEM((3, CONV_HALO, DN_W), BF16),
            pltpu.VMEM((DN_HEADS, DN_HEAD_DIM, DN_HEAD_DIM), F32),
            pltpu.VMEM((2 * DN_HEADS, CHUNK, DN_HEAD_DIM), F32),
            pltpu.VMEM((2 * DN_HEADS, CHUNK, DN_HEAD_DIM), BF16),
            pltpu.VMEM((3 * DN_HEADS, CHUNK, DN_HEAD_DIM), BF16),
            pltpu.VMEM((3 * DN_HEADS, CHUNK, DN_HEAD_DIM), BF16),
            pltpu.VMEM((3 * DN_HEADS, CHUNK, CHUNK), BF16),
            pltpu.VMEM((3 * DN_HEADS, SUBLANES, DN_HEAD_DIM), F32),
            pltpu.VMEM((2 * DN_HEADS, CHUNK, CHUNK), F32),
            pltpu.VMEM((2 * DN_HEADS, CHUNK, 2 * DN_HEAD_DIM), BF16),
        ],
        compiler_params=_params("parallel", "arbitrary"),
        name="gated_deltanet",
    )(proj, proj, proj, proj, gate, gate_t, conv_w, acol, dtcol, arow, dtrow, norm_w.reshape(1, DN_HEAD_DIM))


def _mix_kernel(att_ref, gdn_ref, ga_ref, gb_ref, x_ref, mod_ref, woa_ref, wob_ref, wout_ref, g_ref, b_ref, o_ref):
    y_a = jnp.dot(att_ref[...], woa_ref[...], preferred_element_type=F32)
    y_b = jnp.dot(gdn_ref[...], wob_ref[...], preferred_element_type=F32)
    merged = jax.nn.sigmoid(ga_ref[...].astype(F32)) * y_a + jax.nn.sigmoid(gb_ref[...].astype(F32)) * y_b
    mixed = jnp.dot(merged.astype(BF16), wout_ref[...], preferred_element_type=F32)
    o_ref[...] = _layer_norm(ALPHA * x_ref[...] + (1.0 + mod_ref[2:3, :]) * mixed, g_ref[...], b_ref[...])


def _mix(att, gdn, proj, x, mod_l, w_oa, w_ob, w_out, ln_g, ln_b):
    bsz, s, d = x.shape
    tm = min(512, s)
    row = lambda b, i: (b, i, 0)
    const = lambda b, i: (0, 0)
    return pl.pallas_call(
        _mix_kernel,
        grid=(bsz, s // tm),
        in_specs=[
            pl.BlockSpec((None, tm, d), row),
            pl.BlockSpec((None, tm, d), row),
            pl.BlockSpec((None, tm, d), lambda b, i: (b, i, COLBLK_GA)),
            pl.BlockSpec((None, tm, d), lambda b, i: (b, i, COLBLK_GB)),
            pl.BlockSpec((None, tm, d), row),
            pl.BlockSpec((None, 6, d), lambda b, i: (b, 0, 0)),
            pl.BlockSpec((d, d), const),
            pl.BlockSpec((d, d), const),
            pl.BlockSpec((d, d), const),
            pl.BlockSpec((1, d), const),
            pl.BlockSpec((1, d), const),
        ],
        out_specs=pl.BlockSpec((None, tm, d), row),
        out_shape=jax.ShapeDtypeStruct((bsz, s, d), F32),
        compiler_params=_params("parallel", "parallel"),
        name="mix_out_ln",
    )(att, gdn, proj, proj, x, mod_l, w_oa, w_ob, w_out, ln_g.reshape(1, d), ln_b.reshape(1, d))


def _ffn_kernel(x_ref, mod_ref, w1_ref, b1_ref, w2_ref, b2_ref, g_ref, b_ref, o_ref, u_scr, acc):
    f = pl.program_id(2)

    @pl.when(f == 0)
    def _():
        u_scr[...] = (x_ref[...] * (1.0 + mod_ref[4:5, :]) + mod_ref[3:4, :]).astype(BF16)
        acc[...] = jnp.zeros_like(acc)

    h = jnp.dot(u_scr[...], w1_ref[...], preferred_element_type=F32) + b1_ref[...]
    h = jnp.square(jnp.maximum(h, 0.0))
    acc[...] += jnp.dot(h.astype(BF16), w2_ref[...], preferred_element_type=F32)

    @pl.when(f == pl.num_programs(2) - 1)
    def _():
        y = acc[...] + b2_ref[...]
        o_ref[...] = _layer_norm(ALPHA * x_ref[...] + (1.0 + mod_ref[5:6, :]) * y, g_ref[...], b_ref[...])


def _ffn(x, mod_l, w1, b1, w2, b2, ln_g, ln_b):
    bsz, s, d = x.shape
    dff = w1.shape[1]
    tm = min(1024, s)
    tf = 1024
    row = lambda b, i, f: (b, i, 0)
    const = lambda b, i, f: (0, 0)
    return pl.pallas_call(
        _ffn_kernel,
        grid=(bsz, s // tm, dff // tf),
        in_specs=[
            pl.BlockSpec((None, tm, d), row),
            pl.BlockSpec((None, 6, d), lambda b, i, f: (b, 0, 0)),
            pl.BlockSpec((d, tf), lambda b, i, f: (0, f)),
            pl.BlockSpec((1, tf), lambda b, i, f: (0, f)),
            pl.BlockSpec((tf, d), lambda b, i, f: (f, 0)),
            pl.BlockSpec((1, d), const),
            pl.BlockSpec((1, d), const),
            pl.BlockSpec((1, d), const),
        ],
        out_specs=pl.BlockSpec((None, tm, d), row),
        out_shape=jax.ShapeDtypeStruct((bsz, s, d), F32),
        scratch_shapes=[pltpu.VMEM((tm, d), BF16), pltpu.VMEM((tm, d), F32)],
        compiler_params=_params("parallel", "parallel", "arbitrary"),
        name="ffn_ln",
    )(x, mod_l, w1, b1.reshape(1, dff), w2, b2.reshape(1, d), ln_g.reshape(1, d), ln_b.reshape(1, d))


def kernel(x, c, w_ada, b_ada, w_in, conv_w, a_log, dt_bias, sinks, dn_norm_w, w_oa, w_ob, w_out,
           ln1_g, ln1_b, w_ff1, b_ff1, w_ff2, b_ff2, ln2_g, ln2_b):
    depth = w_ada.shape[0]
    bsz, s, d = x.shape
    mod = _ada_mod(c, w_ada, b_ada).reshape(depth, bsz, 6, d)
    gate_lo = ATT_Q_W + 2 * ATT_KV_W + 4 * DN_W
    gate_hi = gate_lo + GATE_LANES
    w_gate_f32 = w_in[:, :, gate_lo:gate_hi]
    w_gate_all = jnp.pad(w_gate_f32, ((0, 0), (0, 0), (0, LANES - GATE_LANES))).astype(BF16)
    w_gate_t_all = jnp.swapaxes(w_gate_f32, 1, 2).astype(BF16)
    for l in range(depth):
        wl = w_in[l]
        w_main = jnp.concatenate(
            [wl[:, :ATT_Q_W] * (ATT_HEAD_DIM ** -0.5), wl[:, ATT_Q_W + 2 * ATT_KV_W:gate_lo], wl[:, gate_hi:],
             wl[:, ATT_Q_W:ATT_Q_W + 2 * ATT_KV_W]], axis=1).astype(BF16)
        proj, gate, gate_t = _in_proj(x, mod[l], w_main, w_gate_all[l], w_gate_t_all[l])
        att = _attention(proj, sinks[l])
        gdn = _gated_deltanet(proj, gate, gate_t, conv_w[l], a_log[l], dt_bias[l], dn_norm_w[l])
        x = _mix(att, gdn, proj, x, mod[l], w_oa[l].astype(BF16), w_ob[l].astype(BF16), w_out[l].astype(BF16),
                 ln1_g[l], ln1_b[l])
        x = _ffn(x, mod[l], w_ff1[l].astype(BF16), b_ff1[l], w_ff2[l].astype(BF16), b_ff2[l], ln2_g[l], ln2_b[l])
    return x
```

```python
import functools

import jax
import jax.numpy as jnp
from jax import lax
from jax.experimental import pallas as pl
from jax.experimental.pallas import tpu as pltpu

F32 = jnp.float32
BF16 = jnp.bfloat16

D_MODEL = 1024
DEPTH = 4
ATT_HEADS = 16
ATT_KV_HEADS = 4
ATT_GROUP = ATT_HEADS // ATT_KV_HEADS
ATT_HEAD_DIM = 64
WINDOW = 128
DN_HEADS = 8
DN_HEAD_DIM = 128
CONV_K = 4
CHUNK = 64
D_FF = 4 * D_MODEL
ATT_Q_W = ATT_HEADS * ATT_HEAD_DIM
ATT_KV_W = ATT_KV_HEADS * ATT_HEAD_DIM
DN_W = DN_HEADS * DN_HEAD_DIM
ALPHA = (2 * DEPTH) ** 0.25
LN_EPS = 1e-5
RMS_EPS = 1e-6

LANES = 128
SUBLANES = 8
CONV_HALO = 16
GATE_LANES = 2 * DN_HEADS
N_MAIN = ATT_Q_W + 4 * DN_W + 2 * D_MODEL + 2 * ATT_KV_W
COLBLK_Q, COLBLK_DQ, COLBLK_DK, COLBLK_DV, COLBLK_Z, COLBLK_GA, COLBLK_GB = range(7)
COLBLK_K = (ATT_Q_W + 4 * DN_W + 2 * D_MODEL) // ATT_KV_W
COLBLK_V = COLBLK_K + 1
VMEM_LIMIT = 48 * 1024 * 1024
GDN_VMEM_LIMIT = 56 * 1024 * 1024


def _params(*sem, vmem_limit=VMEM_LIMIT):
    return pltpu.CompilerParams(dimension_semantics=sem, vmem_limit_bytes=vmem_limit)


def _dot(a, b):
    return jnp.dot(a.astype(BF16), b.astype(BF16), preferred_element_type=F32)


def _dot_nt(a, b):
    return lax.dot_general(a.astype(BF16), b.astype(BF16), (((1,), (1,)), ((), ())), preferred_element_type=F32)


def _dot_tn(a, b):
    return lax.dot_general(a.astype(BF16), b.astype(BF16), (((0,), (0,)), ((), ())), preferred_element_type=F32)


def _dot_f32(a, b):
    return jnp.dot(a, b, precision=lax.Precision.HIGHEST, preferred_element_type=F32)


def _silu(t):
    half = 0.5 * t
    return half + half * jnp.tanh(half)


def _layer_norm(t, g, b):
    mu = jnp.mean(t, axis=-1, keepdims=True)
    tc = t - mu
    var = jnp.mean(tc * tc, axis=-1, keepdims=True)
    return tc * lax.rsqrt(var + LN_EPS) * g + b


def _ada_kernel(c_ref, w_ref, b_ref, o_ref):
    ca = _silu(c_ref[...])
    o_ref[...] = _dot(ca, w_ref[...]) + b_ref[...]


def _ada_mod(c, w_ada, b_ada):
    depth, d, n = w_ada.shape
    bsz = c.shape[0]
    tn = 1536
    return pl.pallas_call(
        _ada_kernel,
        grid=(depth, n // tn),
        in_specs=[
            pl.BlockSpec((bsz, d), lambda l, j: (0, 0)),
            pl.BlockSpec((None, d, tn), lambda l, j: (l, 0, j)),
            pl.BlockSpec((None, 1, tn), lambda l, j: (l, 0, j)),
        ],
        out_specs=pl.BlockSpec((None, bsz, tn), lambda l, j: (l, 0, j)),
        out_shape=jax.ShapeDtypeStruct((depth, bsz, n), F32),
        compiler_params=_params("parallel", "parallel"),
        name="ada_mod",
    )(c, w_ada, b_ada.reshape(depth, 1, n))


def _inproj_kernel(x_ref, mod_ref, w_ref, wg_ref, wgt_ref, proj_ref, gate_ref, gatet_ref, u_scr):
    j = pl.program_id(2)

    @pl.when(j == 0)
    def _():
        u = (x_ref[...] * (1.0 + mod_ref[1:2, :]) + mod_ref[0:1, :]).astype(BF16)
        u_scr[...] = u
        gate_ref[...] = jnp.dot(u, wg_ref[...], preferred_element_type=F32)
        for ci in range(u.shape[0] // CHUNK):
            uc = u[ci * CHUNK:(ci + 1) * CHUNK]
            gatet_ref[ci] = lax.dot_general(wgt_ref[...], uc, (((1,), (1,)), ((), ())), preferred_element_type=F32)

    proj_ref[...] = jnp.dot(u_scr[...], w_ref[...], preferred_element_type=F32).astype(BF16)


def _in_proj(x, mod_l, w_main, w_gate, w_gate_t):
    bsz, s, d = x.shape
    tm = min(1024, s)
    tn = 1536
    return pl.pallas_call(
        _inproj_kernel,
        grid=(bsz, s // tm, N_MAIN // tn),
        in_specs=[
            pl.BlockSpec((None, tm, d), lambda b, i, j: (b, i, 0)),
            pl.BlockSpec((None, 6, d), lambda b, i, j: (b, 0, 0)),
            pl.BlockSpec((d, tn), lambda b, i, j: (0, j)),
            pl.BlockSpec((d, LANES), lambda b, i, j: (0, 0)),
            pl.BlockSpec((GATE_LANES, d), lambda b, i, j: (0, 0)),
        ],
        out_specs=[
            pl.BlockSpec((None, tm, tn), lambda b, i, j: (b, i, j)),
            pl.BlockSpec((None, tm, LANES), lambda b, i, j: (b, i, 0)),
            pl.BlockSpec((None, tm // CHUNK, GATE_LANES, CHUNK), lambda b, i, j: (b, i, 0, 0)),
        ],
        out_shape=[
            jax.ShapeDtypeStruct((bsz, s, N_MAIN), BF16),
            jax.ShapeDtypeStruct((bsz, s, LANES), F32),
            jax.ShapeDtypeStruct((bsz, s // CHUNK, GATE_LANES, CHUNK), F32),
        ],
        scratch_shapes=[pltpu.VMEM((tm, d), BF16)],
        compiler_params=_params("parallel", "parallel", "arbitrary"),
        name="in_proj",
    )(x, mod_l, w_main, w_gate, w_gate_t)


def _attn_kernel(sink_ref, q_ref, kp_ref, kc_ref, vp_ref, vc_ref, o_ref, *, tq):
    i = pl.program_id(1)
    nsub = tq // WINDOW
    hd, grp, kvh = ATT_HEAD_DIM, ATT_GROUP, ATT_KV_HEADS
    assert 2 * hd == LANES and grp == 4
    kfull = jnp.concatenate([kp_ref[...], kc_ref[...]], axis=0).astype(F32)
    vfull = jnp.concatenate([vp_ref[...], vc_ref[...]], axis=0).astype(F32)
    lane = lax.broadcasted_iota(jnp.int32, (kfull.shape[0], LANES), 1)
    lo_mask = lane < hd

    def lo_hi(full, kv):
        blk = full[:, (kv // 2) * LANES:(kv // 2 + 1) * LANES]
        swapped = pltpu.roll(blk, hd, axis=1)
        src_lo, src_hi = (blk, swapped) if kv % 2 == 0 else (swapped, blk)
        return jnp.where(lo_mask, src_lo, 0.0).astype(BF16), jnp.where(lo_mask, 0.0, src_hi).astype(BF16)

    qi = lax.broadcasted_iota(jnp.int32, (WINDOW, 2 * WINDOW), 0)
    si = lax.broadcasted_iota(jnp.int32, (WINDOW, 2 * WINDOW), 1)
    diff = qi + WINDOW - si
    band = (diff >= 0) & (diff < WINDOW)
    band0 = band & ((si >= WINDOW) | (i > 0))
    neg_inf = jnp.float32(-jnp.inf)

    kv_lo_hi = [(lo_hi(kfull, kv), lo_hi(vfull, kv)) for kv in range(kvh)]
    units = [(kv, s) for kv in range(kvh) for s in range(nsub)]

    def scores(kv, s):
        c0 = kv * grp * hd
        rows = slice(s * WINDOW, (s + 1) * WINDOW)
        band_rows = slice(s * WINDOW, (s + 2) * WINDOW)
        (k_lo, k_hi), _ = kv_lo_hi[kv]
        q2 = jnp.concatenate([q_ref[rows, c0:c0 + LANES], q_ref[rows, c0 + LANES:c0 + 2 * LANES]], axis=0)
        sc_lo = lax.dot_general(q2, k_lo[band_rows], (((1,), (1,)), ((), ())), preferred_element_type=F32)
        sc_hi = lax.dot_general(q2, k_hi[band_rows], (((1,), (1,)), ((), ())), preferred_element_type=F32)
        return sc_lo[:WINDOW], sc_hi[:WINDOW], sc_lo[WINDOW:], sc_hi[WINDOW:]

    sc_next = scores(*units[0])
    for n, (kv, s) in enumerate(units):
        sc_cur = sc_next
        if n + 1 < len(units):
            sc_next = scores(*units[n + 1])
        c0 = kv * grp * hd
        rows = slice(s * WINDOW, (s + 1) * WINDOW)
        band_rows = slice(s * WINDOW, (s + 2) * WINDOW)
        _, (v_lo, v_hi) = kv_lo_hi[kv]
        valid = band0 if s == 0 else band
        probs = []
        for g, sc_g in enumerate(sc_cur):
            sg = jnp.where(valid, sc_g, neg_inf)
            sink = sink_ref[kv * grp + g]
            m = jnp.maximum(jnp.max(sg, axis=-1, keepdims=True), sink)
            p = jnp.exp(sg - m)
            denom = jnp.sum(p, axis=-1, keepdims=True) + jnp.exp(sink - m)
            probs.append((p / denom).astype(BF16))
        pcat = jnp.concatenate([jnp.concatenate(probs[0:2], axis=1), jnp.concatenate(probs[2:4], axis=1)], axis=0)
        vcat = jnp.concatenate([v_lo[band_rows], v_hi[band_rows]], axis=0)
        o = jnp.dot(pcat, vcat, preferred_element_type=F32)
        o_ref[rows, c0:c0 + LANES] = o[:WINDOW].astype(BF16)
        o_ref[rows, c0 + LANES:c0 + 2 * LANES] = o[WINDOW:].astype(BF16)


def _attention(proj, sinks):
    bsz, s, _ = proj.shape
    tq = min(512, s)
    sub = tq // WINDOW
    kern = functools.partial(_attn_kernel, tq=tq)
    return pl.pallas_call(
        kern,
        grid=(bsz, s // tq),
        in_specs=[
            pl.BlockSpec(memory_space=pltpu.SMEM),
            pl.BlockSpec((None, tq, ATT_Q_W), lambda b, i: (b, i, COLBLK_Q)),
            pl.BlockSpec((None, WINDOW, ATT_KV_W), lambda b, i: (b, jnp.maximum(i * sub - 1, 0), COLBLK_K)),
            pl.BlockSpec((None, tq, ATT_KV_W), lambda b, i: (b, i, COLBLK_K)),
            pl.BlockSpec((None, WINDOW, ATT_KV_W), lambda b, i: (b, jnp.maximum(i * sub - 1, 0), COLBLK_V)),
            pl.BlockSpec((None, tq, ATT_KV_W), lambda b, i: (b, i, COLBLK_V)),
        ],
        out_specs=pl.BlockSpec((None, tq, ATT_Q_W), lambda b, i: (b, i, 0)),
        out_shape=jax.ShapeDtypeStruct((bsz, s, ATT_Q_W), BF16),
        compiler_params=_params("parallel", "parallel"),
        name="swa_attention",
    )(sinks, proj, proj, proj, proj, proj)


def _gdn_kernel(dq_ref, dk_ref, dv_ref, z_ref, gate_ref, gatet_ref, convw_ref, acol_ref, dtcol_ref,
                arow_ref, dtrow_ref, nw_ref, o_ref, state, u_s, w_s,
                hqdec_s, hkdec_s, hintra_s, hlast_s, hm_s, hrhs_s, *, ts):
    nh, hd = DN_HEADS, DN_HEAD_DIM
    halo = CONV_HALO
    xrefs = (dq_ref, dk_ref, dv_ref)
    state[...] = jnp.zeros_like(state)

    sr = lax.broadcasted_iota(jnp.int32, ((CONV_K - 1) * CHUNK, CHUNK + halo), 0)
    sc = lax.broadcasted_iota(jnp.int32, ((CONV_K - 1) * CHUNK, CHUNK + halo), 1)
    shift_sel = (sc == (sr & (CHUNK - 1)) + halo - 1 - (sr >> 6)).astype(BF16)

    ri = lax.broadcasted_iota(jnp.int32, (CHUNK, CHUNK), 0)
    ci = lax.broadcasted_iota(jnp.int32, (CHUNK, CHUNK), 1)
    causal = ri >= ci
    strict = ri > ci
    tri_lo = causal.astype(F32)
    tri_up = (ri <= ci).astype(F32)
    eye = (ri == ci).astype(F32)
    neg_inf = jnp.float32(-jnp.inf)

    heads = range(nh)
    hsl = [slice(h * hd, (h + 1) * hd) for h in heads]

    def slot3(c):
        return (c % 3 if isinstance(c, int) else lax.rem(c, 3)) * nh

    def prep(c):
        sb = (c & 1) * nh
        sb3 = slot3(c)
        r0 = pl.multiple_of(c * CHUNK, CHUNK)
        gcols = gate_ref[pl.ds(r0, CHUNK), :]
        beta_cols = jax.nn.sigmoid(gcols)
        g_cols = -jnp.exp(acol_ref[...]) * jax.nn.softplus(gcols + dtcol_ref[...])
        gc_cols = _dot_f32(tri_lo, g_cols)
        grows = gatet_ref[c]
        g_rows = -jnp.exp(arow_ref[...]) * jax.nn.softplus(grows + dtrow_ref[...])
        gc_rows = _dot_f32(g_rows, tri_up)
        yield

        qs, ks, vs = [], [], []
        for hp in range(nh // 2):
            if hp:
                yield
            lanes = slice(2 * hp * hd, (2 * hp + 2) * hd)
            xs = []
            for a in range(3):
                cs = slice(a * DN_W + 2 * hp * hd, a * DN_W + (2 * hp + 2) * hd)
                if isinstance(c, int) and c == 0:
                    win = jnp.concatenate([jnp.zeros((halo, 2 * hd), BF16), xrefs[a][0:CHUNK, lanes]], axis=0)
                else:
                    win = xrefs[a][pl.ds(pl.multiple_of(r0 - halo, halo), CHUNK + halo), lanes]
                back = jnp.dot(shift_sel, win, preferred_element_type=F32)
                acc = win[halo:halo + CHUNK].astype(F32) * convw_ref[CONV_K - 1:CONV_K, cs]
                for jj in range(CONV_K - 1):
                    sh = CONV_K - 1 - jj
                    acc = acc + back[(sh - 1) * CHUNK:sh * CHUNK] * convw_ref[jj:jj + 1, cs]
                xs.append(_silu(acc))
            for hh in range(2):
                q, k, v = (x[:, hh * hd:(hh + 1) * hd] for x in xs)
                qs.append(q * lax.rsqrt(jnp.sum(q * q, axis=-1, keepdims=True) + RMS_EPS) * (hd ** -0.5))
                ks.append(k * lax.rsqrt(jnp.sum(k * k, axis=-1, keepdims=True) + RMS_EPS))
                vs.append(v)
        yield

        gcbs = [jnp.broadcast_to(gc_cols[:, nh + h:nh + h + 1], (CHUNK, hd)) for h in heads]
        gc_lasts = [gc_rows[nh + h:nh + h + 1, CHUNK - 1:CHUNK] for h in heads]
        decays = [jnp.exp(jnp.where(causal, gcbs[h][:, :CHUNK] - gc_rows[nh + h:nh + h + 1, :], neg_inf)) for h in heads]
        egcs = [jnp.exp(gcbs[h]) for h in heads]
        betas = [jnp.broadcast_to(beta_cols[:, h:h + 1], (CHUNK, hd)) for h in heads]
        kbs = [ks[h] * betas[h] for h in heads]
        vbs = [vs[h] * betas[h] for h in heads]
        for h in heads:
            hqdec_s[sb3 + h] = (qs[h] * egcs[h]).astype(BF16)
            hkdec_s[sb3 + h] = (ks[h] * jnp.exp(gc_lasts[h] - gcbs[h])).astype(BF16)
            hlast_s[sb3 + h] = jnp.broadcast_to(jnp.exp(gc_lasts[h]), (SUBLANES, hd))
            hrhs_s[sb + h] = jnp.concatenate([vbs[h], kbs[h] * egcs[h]], axis=1).astype(BF16)
        kbq = [jnp.concatenate([kbs[h], qs[h]], axis=0).astype(BF16) for h in heads]
        kbf = [ks[h].astype(BF16) for h in heads]
        yield

        kkqk = [_dot_nt(kbq[h], kbf[h]) for h in heads]
        yield
        for h in heads:
            hintra_s[sb3 + h] = (kkqk[h][CHUNK:] * decays[h]).astype(BF16)
            hm_s[sb + h] = -jnp.where(strict, kkqk[h][:CHUNK] * decays[h], 0.0)

    def solve(c):
        sb = (c & 1) * nh
        ms = [hm_s[sb + h] for h in heads]
        rhss = [hrhs_s[sb + h] for h in heads]
        xinvs = [eye + ms[h] for h in heads]
        ms = [_dot(ms[h], ms[h]) for h in heads]
        yield
        npow = 2
        while npow < CHUNK:
            if npow * 2 < CHUNK:
                both = [_dot(jnp.concatenate([xinvs[h], ms[h]], axis=0), ms[h]) for h in heads]
                yield
                xinvs = [xinvs[h] + both[h][:CHUNK] for h in heads]
                ms = [both[h][CHUNK:] for h in heads]
            else:
                tails = [_dot(xinvs[h], ms[h]) for h in heads]
                yield
                xinvs = [xinvs[h] + tails[h] for h in heads]
            npow *= 2
        sols = [_dot(xinvs[h], rhss[h]) for h in heads]
        yield
        for h in heads:
            u_s[sb + h] = sols[h][:, :hd]
            w_s[sb + h] = sols[h][:, hd:].astype(BF16)

    def recur(c):
        r0 = pl.multiple_of(c * CHUNK, CHUNK)
        sb = (c & 1) * nh
        sb3 = slot3(c)
        sts = [state[h] for h in heads]
        us = [u_s[sb + h] for h in heads]
        kdecs = [hkdec_s[sb3 + h] for h in heads]
        intras = [hintra_s[sb3 + h] for h in heads]
        lasts = [hlast_s[sb3 + h][0:1, :] for h in heads]
        ws_qs = [_dot(jnp.concatenate([w_s[sb + h], hqdec_s[sb3 + h]], axis=0), sts[h]) for h in heads]
        yield
        v_news = [us[h] - ws_qs[h][:CHUNK] for h in heads]
        upds = [_dot_tn(kdecs[h], v_news[h]) for h in heads]
        outs = [ws_qs[h][CHUNK:] + _dot(intras[h], v_news[h]) for h in heads]
        yield
        for h in heads:
            state[h] = sts[h] * lasts[h] + upds[h]
        for h in heads:
            o = outs[h]
            o = o * lax.rsqrt(jnp.mean(o * o, axis=-1, keepdims=True) + RMS_EPS) * nw_ref[...]
            o = o * _silu(z_ref[pl.ds(r0, CHUNK), hsl[h]].astype(F32))
            o_ref[pl.ds(r0, CHUNK), hsl[h]] = o.astype(BF16)

    def run_interleaved(*gens):
        live = list(gens)
        while live:
            for g in list(live):
                try:
                    next(g)
                except StopIteration:
                    live.remove(g)

    nchunk = ts // CHUNK
    assert nchunk >= 2
    run_interleaved(prep(0))
    run_interleaved(solve(0), prep(1))

    def chunk_body(c, carry):
        run_interleaved(recur(c), solve(c + 1), prep(c + 2))
        return carry

    lax.fori_loop(0, nchunk - 2, chunk_body, 0)
    run_interleaved(recur(nchunk - 2), solve(nchunk - 1))
    run_interleaved(recur(nchunk - 1))


def _gated_deltanet(proj, gate, gate_t, conv_w, a_log, dt_bias, norm_w):
    bsz, s, _ = proj.shape
    ts = s
    nh = DN_HEADS
    zpad = jnp.zeros((nh,), F32)
    acol = jnp.concatenate([zpad, a_log, jnp.zeros((LANES - 2 * nh,), F32)]).reshape(1, LANES)
    dtcol = jnp.concatenate([zpad, dt_bias, jnp.zeros((LANES - 2 * nh,), F32)]).reshape(1, LANES)
    arow = jnp.broadcast_to(jnp.concatenate([zpad, a_log]).reshape(GATE_LANES, 1), (GATE_LANES, CHUNK))
    dtrow = jnp.broadcast_to(jnp.concatenate([zpad, dt_bias]).reshape(GATE_LANES, 1), (GATE_LANES, CHUNK))
    kern = functools.partial(_gdn_kernel, ts=ts)
    const = lambda b: (0, 0)
    return pl.pallas_call(
        kern,
        grid=(bsz,),
        in_specs=[
            pl.BlockSpec((None, ts, DN_W), lambda b: (b, 0, COLBLK_DQ)),
            pl.BlockSpec((None, ts, DN_W), lambda b: (b, 0, COLBLK_DK)),
            pl.BlockSpec((None, ts, DN_W), lambda b: (b, 0, COLBLK_DV)),
            pl.BlockSpec((None, ts, DN_W), lambda b: (b, 0, COLBLK_Z)),
            pl.BlockSpec((None, ts, LANES), lambda b: (b, 0, 0)),
            pl.BlockSpec((None, ts // CHUNK, GATE_LANES, CHUNK), lambda b: (b, 0, 0, 0)),
            pl.BlockSpec((CONV_K, 3 * DN_W), const),
            pl.BlockSpec((1, LANES), const),
            pl.BlockSpec((1, LANES), const),
            pl.BlockSpec((GATE_LANES, CHUNK), const),
            pl.BlockSpec((GATE_LANES, CHUNK), const),
            pl.BlockSpec((1, DN_HEAD_DIM), const),
        ],
        out_specs=pl.BlockSpec((None, ts, DN_W), lambda b: (b, 0, 0)),
        out_shape=jax.ShapeDtypeStruct((bsz, s, DN_W), BF16),
        scratch_shapes=[
            pltpu.VMEM((DN_HEADS, DN_HEAD_DIM, DN_HEAD_DIM), F32),
            pltpu.VMEM((2 * DN_HEADS, CHUNK, DN_HEAD_DIM), F32),
            pltpu.VMEM((2 * DN_HEADS, CHUNK, DN_HEAD_DIM), BF16),
            pltpu.VMEM((3 * DN_HEADS, CHUNK, DN_HEAD_DIM), BF16),
            pltpu.VMEM((3 * DN_HEADS, CHUNK, DN_HEAD_DIM), BF16),
            pltpu.VMEM((3 * DN_HEADS, CHUNK, CHUNK), BF16),
            pltpu.VMEM((3 * DN_HEADS, SUBLANES, DN_HEAD_DIM), F32),
            pltpu.VMEM((2 * DN_HEADS, CHUNK, CHUNK), F32),
            pltpu.VMEM((2 * DN_HEADS, CHUNK, 2 * DN_HEAD_DIM), BF16),
        ],
        compiler_params=_params("parallel", vmem_limit=GDN_VMEM_LIMIT),
        name="gated_deltanet",
    )(proj, proj, proj, proj, gate, gate_t, conv_w, acol, dtcol, arow, dtrow, norm_w.reshape(1, DN_HEAD_DIM))


def _mix_kernel(att_ref, gdn_ref, ga_ref, gb_ref, x_ref, mod_ref, woa_ref, wob_ref, wout_ref, g_ref, b_ref, o_ref):
    y_a = jnp.dot(att_ref[...], woa_ref[...], preferred_element_type=F32)
    y_b = jnp.dot(gdn_ref[...], wob_ref[...], preferred_element_type=F32)
    merged = jax.nn.sigmoid(ga_ref[...].astype(F32)) * y_a + jax.nn.sigmoid(gb_ref[...].astype(F32)) * y_b
    mixed = jnp.dot(merged.astype(BF16), wout_ref[...], preferred_element_type=F32)
    o_ref[...] = _layer_norm(ALPHA * x_ref[...] + (1.0 + mod_ref[2:3, :]) * mixed, g_ref[...], b_ref[...])


def _mix(att, gdn, proj, x, mod_l, w_oa, w_ob, w_out, ln_g, ln_b):
    bsz, s, d = x.shape
    tm = min(512, s)
    row = lambda b, i: (b, i, 0)
    const = lambda b, i: (0, 0)
    return pl.pallas_call(
        _mix_kernel,
        grid=(bsz, s // tm),
        in_specs=[
            pl.BlockSpec((None, tm, d), row),
            pl.BlockSpec((None, tm, d), row),
            pl.BlockSpec((None, tm, d), lambda b, i: (b, i, COLBLK_GA)),
            pl.BlockSpec((None, tm, d), lambda b, i: (b, i, COLBLK_GB)),
            pl.BlockSpec((None, tm, d), row),
            pl.BlockSpec((None, 6, d), lambda b, i: (b, 0, 0)),
            pl.BlockSpec((d, d), const),
            pl.BlockSpec((d, d), const),
            pl.BlockSpec((d, d), const),
            pl.BlockSpec((1, d), const),
            pl.BlockSpec((1, d), const),
        ],
        out_specs=pl.BlockSpec((None, tm, d), row),
        out_shape=jax.ShapeDtypeStruct((bsz, s, d), F32),
        compiler_params=_params("parallel", "parallel"),
        name="mix_out_ln",
    )(att, gdn, proj, proj, x, mod_l, w_oa, w_ob, w_out, ln_g.reshape(1, d), ln_b.reshape(1, d))


def _ffn_kernel(x_ref, mod_ref, w1_ref, b1_ref, w2_ref, b2_ref, g_ref, b_ref, o_ref, u_scr, acc):
    f = pl.program_id(2)

    @pl.when(f == 0)
    def _():
        u_scr[...] = (x_ref[...] * (1.0 + mod_ref[4:5, :]) + mod_ref[3:4, :]).astype(BF16)
        acc[...] = jnp.zeros_like(acc)

    h = jnp.dot(u_scr[...], w1_ref[...], preferred_element_type=F32) + b1_ref[...]
    h = jnp.square(jnp.maximum(h, 0.0))
    acc[...] += jnp.dot(h.astype(BF16), w2_ref[...], preferred_element_type=F32)

    @pl.when(f == pl.num_programs(2) - 1)
    def _():
        y = acc[...] + b2_ref[...]
        o_ref[...] = _layer_norm(ALPHA * x_ref[...] + (1.0 + mod_ref[5:6, :]) * y, g_ref[...], b_ref[...])


def _ffn(x, mod_l, w1, b1, w2, b2, ln_g, ln_b):
    bsz, s, d = x.shape
    dff = w1.shape[1]
    tm = min(1024, s)
    tf = 1024
    row = lambda b, i, f: (b, i, 0)
    const = lambda b, i, f: (0, 0)
    return pl.pallas_call(
        _ffn_kernel,
        grid=(bsz, s // tm, dff // tf),
        in_specs=[
            pl.BlockSpec((None, tm, d), row),
            pl.BlockSpec((None, 6, d), lambda b, i, f: (b, 0, 0)),
            pl.BlockSpec((d, tf), lambda b, i, f: (0, f)),
            pl.BlockSpec((1, tf), lambda b, i, f: (0, f)),
            pl.BlockSpec((tf, d), lambda b, i, f: (f, 0)),
            pl.BlockSpec((1, d), const),
            pl.BlockSpec((1, d), const),
            pl.BlockSpec((1, d), const),
        ],
        out_specs=pl.BlockSpec((None, tm, d), row),
        out_shape=jax.ShapeDtypeStruct((bsz, s, d), F32),
        scratch_shapes=[pltpu.VMEM((tm, d), BF16), pltpu.VMEM((tm, d), F32)],
        compiler_params=_params("parallel", "parallel", "arbitrary"),
        name="ffn_ln",
    )(x, mod_l, w1, b1.reshape(1, dff), w2, b2.reshape(1, d), ln_g.reshape(1, d), ln_b.reshape(1, d))


def kernel(x, c, w_ada, b_ada, w_in, conv_w, a_log, dt_bias, sinks, dn_norm_w, w_oa, w_ob, w_out,
           ln1_g, ln1_b, w_ff1, b_ff1, w_ff2, b_ff2, ln2_g, ln2_b):
    depth = w_ada.shape[0]
    bsz, s, d = x.shape
    mod = _ada_mod(c, w_ada, b_ada).reshape(depth, bsz, 6, d)
    gate_lo = ATT_Q_W + 2 * ATT_KV_W + 4 * DN_W
    gate_hi = gate_lo + GATE_LANES
    w_gate_f32 = w_in[:, :, gate_lo:gate_hi]
    w_gate_all = jnp.pad(w_gate_f32, ((0, 0), (0, 0), (0, LANES - GATE_LANES))).astype(BF16)
    w_gate_t_all = jnp.swapaxes(w_gate_f32, 1, 2).astype(BF16)
    for l in range(depth):
        wl = w_in[l]
        w_main = jnp.concatenate(
            [wl[:, :ATT_Q_W] * (ATT_HEAD_DIM ** -0.5), wl[:, ATT_Q_W + 2 * ATT_KV_W:gate_lo], wl[:, gate_hi:],
             wl[:, ATT_Q_W:ATT_Q_W + 2 * ATT_KV_W]], axis=1).astype(BF16)
        proj, gate, gate_t = _in_proj(x, mod[l], w_main, w_gate_all[l], w_gate_t_all[l])
        att = _attention(proj, sinks[l])
        gdn = _gated_deltanet(proj, gate, gate_t, conv_w[l], a_log[l], dt_bias[l], dn_norm_w[l])
        x = _mix(att, gdn, proj, x, mod[l], w_oa[l].astype(BF16), w_ob[l].astype(BF16), w_out[l].astype(BF16),
                 ln1_g[l], ln1_b[l])
        x = _ffn(x, mod[l], w_ff1[l].astype(BF16), b_ff1[l], w_ff2[l].astype(BF16), b_ff2[l], ln2_g[l], ln2_b[l])
    return x
```

```python
import functools

import jax
import jax.numpy as jnp
from jax import lax
from jax.experimental import pallas as pl
from jax.experimental.pallas import tpu as pltpu

F32 = jnp.float32
BF16 = jnp.bfloat16

D_MODEL = 1024
DEPTH = 4
ATT_HEADS = 16
ATT_KV_HEADS = 4
ATT_GROUP = ATT_HEADS // ATT_KV_HEADS
ATT_HEAD_DIM = 64
WINDOW = 128
DN_HEADS = 8
DN_HEAD_DIM = 128
CONV_K = 4
CHUNK = 64
D_FF = 4 * D_MODEL
ATT_Q_W = ATT_HEADS * ATT_HEAD_DIM
ATT_KV_W = ATT_KV_HEADS * ATT_HEAD_DIM
DN_W = DN_HEADS * DN_HEAD_DIM
ALPHA = (2 * DEPTH) ** 0.25
LN_EPS = 1e-5
RMS_EPS = 1e-6

LANES = 128
SUBLANES = 8
CONV_HALO = 16
GATE_LANES = 2 * DN_HEADS
N_MAIN = ATT_Q_W + 4 * DN_W + 2 * D_MODEL + 2 * ATT_KV_W
COLBLK_Q, COLBLK_DQ, COLBLK_DK, COLBLK_DV, COLBLK_Z, COLBLK_GA, COLBLK_GB = range(7)
COLBLK_K = (ATT_Q_W + 4 * DN_W + 2 * D_MODEL) // ATT_KV_W
COLBLK_V = COLBLK_K + 1
VMEM_LIMIT = 48 * 1024 * 1024
GDN_VMEM_LIMIT = 56 * 1024 * 1024


def _params(*sem, vmem_limit=VMEM_LIMIT):
    return pltpu.CompilerParams(dimension_semantics=sem, vmem_limit_bytes=vmem_limit)


def _dot(a, b):
    return jnp.dot(a.astype(BF16), b.astype(BF16), preferred_element_type=F32)


def _dot_nt(a, b):
    return lax.dot_general(a.astype(BF16), b.astype(BF16), (((1,), (1,)), ((), ())), preferred_element_type=F32)


def _dot_tn(a, b):
    return lax.dot_general(a.astype(BF16), b.astype(BF16), (((0,), (0,)), ((), ())), preferred_element_type=F32)


def _dot_f32(a, b):
    return jnp.dot(a, b, precision=lax.Precision.HIGHEST, preferred_element_type=F32)


def _silu(t):
    half = 0.5 * t
    return half + half * jnp.tanh(half)


def _layer_norm(t, g, b):
    mu = jnp.mean(t, axis=-1, keepdims=True)
    tc = t - mu
    var = jnp.mean(tc * tc, axis=-1, keepdims=True)
    return tc * lax.rsqrt(var + LN_EPS) * g + b


def _ada_kernel(c_ref, w_ref, b_ref, o_ref):
    ca = _silu(c_ref[...])
    o_ref[...] = _dot(ca, w_ref[...]) + b_ref[...]


def _ada_mod(c, w_ada, b_ada):
    depth, d, n = w_ada.shape
    bsz = c.shape[0]
    tn = 1536
    return pl.pallas_call(
        _ada_kernel,
        grid=(depth, n // tn),
        in_specs=[
            pl.BlockSpec((bsz, d), lambda l, j: (0, 0)),
            pl.BlockSpec((None, d, tn), lambda l, j: (l, 0, j)),
            pl.BlockSpec((None, 1, tn), lambda l, j: (l, 0, j)),
        ],
        out_specs=pl.BlockSpec((None, bsz, tn), lambda l, j: (l, 0, j)),
        out_shape=jax.ShapeDtypeStruct((depth, bsz, n), F32),
        compiler_params=_params("parallel", "parallel"),
        name="ada_mod",
    )(c, w_ada, b_ada.reshape(depth, 1, n))


def _inproj_kernel(x_ref, mod_ref, w_ref, wg_ref, wgt_ref, proj_ref, gate_ref, gatet_ref, u_scr):
    j = pl.program_id(2)

    @pl.when(j == 0)
    def _():
        u = (x_ref[...] * (1.0 + mod_ref[1:2, :]) + mod_ref[0:1, :]).astype(BF16)
        u_scr[...] = u
        gate_ref[...] = jnp.dot(u, wg_ref[...], preferred_element_type=F32)
        for ci in range(u.shape[0] // CHUNK):
            uc = u[ci * CHUNK:(ci + 1) * CHUNK]
            gatet_ref[ci] = lax.dot_general(wgt_ref[...], uc, (((1,), (1,)), ((), ())), preferred_element_type=F32)

    proj_ref[...] = jnp.dot(u_scr[...], w_ref[...], preferred_element_type=F32).astype(BF16)


def _in_proj(x, mod_l, w_main, w_gate, w_gate_t):
    bsz, s, d = x.shape
    tm = min(1024, s)
    tn = 1536
    return pl.pallas_call(
        _inproj_kernel,
        grid=(bsz, s // tm, N_MAIN // tn),
        in_specs=[
            pl.BlockSpec((None, tm, d), lambda b, i, j: (b, i, 0)),
            pl.BlockSpec((None, 6, d), lambda b, i, j: (b, 0, 0)),
            pl.BlockSpec((d, tn), lambda b, i, j: (0, j)),
            pl.BlockSpec((d, LANES), lambda b, i, j: (0, 0)),
            pl.BlockSpec((GATE_LANES, d), lambda b, i, j: (0, 0)),
        ],
        out_specs=[
            pl.BlockSpec((None, tm, tn), lambda b, i, j: (b, i, j)),
            pl.BlockSpec((None, tm, LANES), lambda b, i, j: (b, i, 0)),
            pl.BlockSpec((None, tm // CHUNK, GATE_LANES, CHUNK), lambda b, i, j: (b, i, 0, 0)),
        ],
        out_shape=[
            jax.ShapeDtypeStruct((bsz, s, N_MAIN), BF16),
            jax.ShapeDtypeStruct((bsz, s, LANES), F32),
            jax.ShapeDtypeStruct((bsz, s // CHUNK, GATE_LANES, CHUNK), F32),
        ],
        scratch_shapes=[pltpu.VMEM((tm, d), BF16)],
        compiler_params=_params("parallel", "parallel", "arbitrary"),
        name="in_proj",
    )(x, mod_l, w_main, w_gate, w_gate_t)


def _attn_kernel(sink_ref, q_ref, kp_ref, kc_ref, vp_ref, vc_ref, o_ref, *, tq):
    i = pl.program_id(1)
    nsub = tq // WINDOW
    hd, grp, kvh = ATT_HEAD_DIM, ATT_GROUP, ATT_KV_HEADS
    assert 2 * hd == LANES and grp == 4
    kfull = jnp.concatenate([kp_ref[...], kc_ref[...]], axis=0).astype(F32)
    vfull = jnp.concatenate([vp_ref[...], vc_ref[...]], axis=0).astype(F32)
    lane = lax.broadcasted_iota(jnp.int32, (kfull.shape[0], LANES), 1)
    lo_mask = lane < hd

    def lo_hi(full, kv):
        blk = full[:, (kv // 2) * LANES:(kv // 2 + 1) * LANES]
        swapped = pltpu.roll(blk, hd, axis=1)
        src_lo, src_hi = (blk, swapped) if kv % 2 == 0 else (swapped, blk)
        return jnp.where(lo_mask, src_lo, 0.0).astype(BF16), jnp.where(lo_mask, 0.0, src_hi).astype(BF16)

    qi = lax.broadcasted_iota(jnp.int32, (WINDOW, 2 * WINDOW), 0)
    si = lax.broadcasted_iota(jnp.int32, (WINDOW, 2 * WINDOW), 1)
    diff = qi + WINDOW - si
    band = (diff >= 0) & (diff < WINDOW)
    band0 = band & ((si >= WINDOW) | (i > 0))
    neg_inf = jnp.float32(-jnp.inf)

    kv_lo_hi = [(lo_hi(kfull, kv), lo_hi(vfull, kv)) for kv in range(kvh)]
    units = [(kv, s) for kv in range(kvh) for s in range(nsub)]

    def scores(kv, s):
        c0 = kv * grp * hd
        rows = slice(s * WINDOW, (s + 1) * WINDOW)
        band_rows = slice(s * WINDOW, (s + 2) * WINDOW)
        (k_lo, k_hi), _ = kv_lo_hi[kv]
        q2 = jnp.concatenate([q_ref[rows, c0:c0 + LANES], q_ref[rows, c0 + LANES:c0 + 2 * LANES]], axis=0)
        sc_lo = lax.dot_general(q2, k_lo[band_rows], (((1,), (1,)), ((), ())), preferred_element_type=F32)
        sc_hi = lax.dot_general(q2, k_hi[band_rows], (((1,), (1,)), ((), ())), preferred_element_type=F32)
        return sc_lo[:WINDOW], sc_hi[:WINDOW], sc_lo[WINDOW:], sc_hi[WINDOW:]

    sc_next = scores(*units[0])
    for n, (kv, s) in enumerate(units):
        sc_cur = sc_next
        if n + 1 < len(units):
            sc_next = scores(*units[n + 1])
        c0 = kv * grp * hd
        rows = slice(s * WINDOW, (s + 1) * WINDOW)
        band_rows = slice(s * WINDOW, (s + 2) * WINDOW)
        _, (v_lo, v_hi) = kv_lo_hi[kv]
        valid = band0 if s == 0 else band
        probs = []
        for g, sc_g in enumerate(sc_cur):
            sg = jnp.where(valid, sc_g, neg_inf)
            sink = sink_ref[kv * grp + g]
            m = jnp.maximum(jnp.max(sg, axis=-1, keepdims=True), sink)
            p = jnp.exp(sg - m)
            denom = jnp.sum(p, axis=-1, keepdims=True) + jnp.exp(sink - m)
            probs.append((p / denom).astype(BF16))
        pcat = jnp.concatenate([jnp.concatenate(probs[0:2], axis=1), jnp.concatenate(probs[2:4], axis=1)], axis=0)
        vcat = jnp.concatenate([v_lo[band_rows], v_hi[band_rows]], axis=0)
        o = jnp.dot(pcat, vcat, preferred_element_type=F32)
        o_ref[rows, c0:c0 + LANES] = o[:WINDOW].astype(BF16)
        o_ref[rows, c0 + LANES:c0 + 2 * LANES] = o[WINDOW:].astype(BF16)


def _attention(proj, sinks):
    bsz, s, _ = proj.shape
    tq = min(1024, s)
    sub = tq // WINDOW
    kern = functools.partial(_attn_kernel, tq=tq)
    return pl.pallas_call(
        kern,
        grid=(bsz, s // tq),
        in_specs=[
            pl.BlockSpec(memory_space=pltpu.SMEM),
            pl.BlockSpec((None, tq, ATT_Q_W), lambda b, i: (b, i, COLBLK_Q)),
            pl.BlockSpec((None, WINDOW, ATT_KV_W), lambda b, i: (b, jnp.maximum(i * sub - 1, 0), COLBLK_K)),
            pl.BlockSpec((None, tq, ATT_KV_W), lambda b, i: (b, i, COLBLK_K)),
            pl.BlockSpec((None, WINDOW, ATT_KV_W), lambda b, i: (b, jnp.maximum(i * sub - 1, 0), COLBLK_V)),
            pl.BlockSpec((None, tq, ATT_KV_W), lambda b, i: (b, i, COLBLK_V)),
        ],
        out_specs=pl.BlockSpec((None, tq, ATT_Q_W), lambda b, i: (b, i, 0)),
        out_shape=jax.ShapeDtypeStruct((bsz, s, ATT_Q_W), BF16),
        compiler_params=_params("parallel", "parallel"),
        name="swa_attention",
    )(sinks, proj, proj, proj, proj, proj)


def _gdn_kernel(dq_ref, dk_ref, dv_ref, z_ref, gate_ref, gatet_ref, convw_ref, acol_ref, dtcol_ref,
                arow_ref, dtrow_ref, nw_ref, o_ref, state, u_s, w_s,
                hqdec_s, hkdec_s, hintra_s, hlast_s, hm_s, hrhs_s, *, ts):
    nh, hd = DN_HEADS, DN_HEAD_DIM
    halo = CONV_HALO
    xrefs = (dq_ref, dk_ref, dv_ref)
    state[...] = jnp.zeros_like(state)

    sr = lax.broadcasted_iota(jnp.int32, ((CONV_K - 1) * CHUNK, CHUNK + halo), 0)
    sc = lax.broadcasted_iota(jnp.int32, ((CONV_K - 1) * CHUNK, CHUNK + halo), 1)
    shift_sel = (sc == (sr & (CHUNK - 1)) + halo - 1 - (sr >> 6)).astype(BF16)

    ri = lax.broadcasted_iota(jnp.int32, (CHUNK, CHUNK), 0)
    ci = lax.broadcasted_iota(jnp.int32, (CHUNK, CHUNK), 1)
    causal = ri >= ci
    strict = ri > ci
    tri_lo = causal.astype(F32)
    tri_up = (ri <= ci).astype(F32)
    eye = (ri == ci).astype(F32)
    neg_inf = jnp.float32(-jnp.inf)

    heads = range(nh)
    hsl = [slice(h * hd, (h + 1) * hd) for h in heads]

    def slot3(c):
        return (c % 3 if isinstance(c, int) else lax.rem(c, 3)) * nh

    def prep(c):
        sb = (c & 1) * nh
        sb3 = slot3(c)
        r0 = pl.multiple_of(c * CHUNK, CHUNK)
        gcols = gate_ref[pl.ds(r0, CHUNK), :]
        beta_cols = jax.nn.sigmoid(gcols)
        g_cols = -jnp.exp(acol_ref[...]) * jax.nn.softplus(gcols + dtcol_ref[...])
        gc_cols = _dot_f32(tri_lo, g_cols)
        grows = gatet_ref[c]
        g_rows = -jnp.exp(arow_ref[...]) * jax.nn.softplus(grows + dtrow_ref[...])
        gc_rows = _dot_f32(g_rows, tri_up)
        yield

        qs, ks, vs = [], [], []
        for hp in range(nh // 2):
            if hp:
                yield
            lanes = slice(2 * hp * hd, (2 * hp + 2) * hd)
            xs = []
            for a in range(3):
                cs = slice(a * DN_W + 2 * hp * hd, a * DN_W + (2 * hp + 2) * hd)
                if isinstance(c, int) and c == 0:
                    win = jnp.concatenate([jnp.zeros((halo, 2 * hd), BF16), xrefs[a][0:CHUNK, lanes]], axis=0)
                else:
                    win = xrefs[a][pl.ds(pl.multiple_of(r0 - halo, halo), CHUNK + halo), lanes]
                back = jnp.dot(shift_sel, win, preferred_element_type=F32)
                acc = win[halo:halo + CHUNK].astype(F32) * convw_ref[CONV_K - 1:CONV_K, cs]
                for jj in range(CONV_K - 1):
                    sh = CONV_K - 1 - jj
                    acc = acc + back[(sh - 1) * CHUNK:sh * CHUNK] * convw_ref[jj:jj + 1, cs]
                xs.append(_silu(acc))
            for hh in range(2):
                q, k, v = (x[:, hh * hd:(hh + 1) * hd] for x in xs)
                qs.append(q * lax.rsqrt(jnp.sum(q * q, axis=-1, keepdims=True) + RMS_EPS) * (hd ** -0.5))
                ks.append(k * lax.rsqrt(jnp.sum(k * k, axis=-1, keepdims=True) + RMS_EPS))
                vs.append(v)
        yield

        gcbs = [jnp.broadcast_to(gc_cols[:, nh + h:nh + h + 1], (CHUNK, hd)) for h in heads]
        gc_lasts = [gc_rows[nh + h:nh + h + 1, CHUNK - 1:CHUNK] for h in heads]
        decays = [jnp.exp(jnp.where(causal, gcbs[h][:, :CHUNK] - gc_rows[nh + h:nh + h + 1, :], neg_inf)) for h in heads]
        egcs = [jnp.exp(gcbs[h]) for h in heads]
        betas = [jnp.broadcast_to(beta_cols[:, h:h + 1], (CHUNK, hd)) for h in heads]
        kbs = [ks[h] * betas[h] for h in heads]
        vbs = [vs[h] * betas[h] for h in heads]
        for h in heads:
            hqdec_s[sb3 + h] = (qs[h] * egcs[h]).astype(BF16)
            hkdec_s[sb3 + h] = (ks[h] * jnp.exp(gc_lasts[h] - gcbs[h])).astype(BF16)
            hlast_s[sb3 + h] = jnp.broadcast_to(jnp.exp(gc_lasts[h]), (SUBLANES, hd))
            hrhs_s[sb + h] = jnp.concatenate([vbs[h], kbs[h] * egcs[h]], axis=1).astype(BF16)
        kbq = [jnp.concatenate([kbs[h], qs[h]], axis=0).astype(BF16) for h in heads]
        kbf = [ks[h].astype(BF16) for h in heads]
        yield

        kkqk = [_dot_nt(kbq[h], kbf[h]) for h in heads]
        yield
        for h in heads:
            hintra_s[sb3 + h] = (kkqk[h][CHUNK:] * decays[h]).astype(BF16)
            hm_s[sb + h] = -jnp.where(strict, kkqk[h][:CHUNK] * decays[h], 0.0)

    def solve(c):
        sb = (c & 1) * nh
        ms = [hm_s[sb + h] for h in heads]
        rhss = [hrhs_s[sb + h] for h in heads]
        xinvs = [eye + ms[h] for h in heads]
        ms = [_dot(ms[h], ms[h]) for h in heads]
        yield
        npow = 2
        while npow < CHUNK:
            if npow * 2 < CHUNK:
                both = [_dot(jnp.concatenate([xinvs[h], ms[h]], axis=0), ms[h]) for h in heads]
                yield
                xinvs = [xinvs[h] + both[h][:CHUNK] for h in heads]
                ms = [both[h][CHUNK:] for h in heads]
            else:
                tails = [_dot(xinvs[h], ms[h]) for h in heads]
                yield
                xinvs = [xinvs[h] + tails[h] for h in heads]
            npow *= 2
        sols = [_dot(xinvs[h], rhss[h]) for h in heads]
        yield
        for h in heads:
            u_s[sb + h] = sols[h][:, :hd]
            w_s[sb + h] = sols[h][:, hd:].astype(BF16)

    def recur(c):
        r0 = pl.multiple_of(c * CHUNK, CHUNK)
        sb = (c & 1) * nh
        sb3 = slot3(c)
        sts = [state[h] for h in heads]
        us = [u_s[sb + h] for h in heads]
        kdecs = [hkdec_s[sb3 + h] for h in heads]
        intras = [hintra_s[sb3 + h] for h in heads]
        lasts = [hlast_s[sb3 + h][0:1, :] for h in heads]
        ws_qs = [_dot(jnp.concatenate([w_s[sb + h], hqdec_s[sb3 + h]], axis=0), sts[h]) for h in heads]
        yield
        v_news = [us[h] - ws_qs[h][:CHUNK] for h in heads]
        upds = [_dot_tn(kdecs[h], v_news[h]) for h in heads]
        outs = [ws_qs[h][CHUNK:] + _dot(intras[h], v_news[h]) for h in heads]
        yield
        for h in heads:
            state[h] = sts[h] * lasts[h] + upds[h]
        for h in heads:
            o = outs[h]
            o = o * lax.rsqrt(jnp.mean(o * o, axis=-1, keepdims=True) + RMS_EPS) * nw_ref[...]
            o = o * _silu(z_ref[pl.ds(r0, CHUNK), hsl[h]].astype(F32))
            o_ref[pl.ds(r0, CHUNK), hsl[h]] = o.astype(BF16)

    def run_interleaved(*gens):
        live = list(gens)
        while live:
            for g in list(live):
                try:
                    next(g)
                except StopIteration:
                    live.remove(g)

    nchunk = ts // CHUNK
    assert nchunk >= 2
    run_interleaved(prep(0))
    run_interleaved(solve(0), prep(1))

    def chunk_body(c, carry):
        run_interleaved(recur(c), solve(c + 1), prep(c + 2))
        return carry

    lax.fori_loop(0, nchunk - 2, chunk_body, 0)
    run_interleaved(recur(nchunk - 2), solve(nchunk - 1))
    run_interleaved(recur(nchunk - 1))


def _gated_deltanet(proj, gate, gate_t, conv_w, a_log, dt_bias, norm_w):
    bsz, s, _ = proj.shape
    ts = s
    nh = DN_HEADS
    zpad = jnp.zeros((nh,), F32)
    acol = jnp.concatenate([zpad, a_log, jnp.zeros((LANES - 2 * nh,), F32)]).reshape(1, LANES)
    dtcol = jnp.concatenate([zpad, dt_bias, jnp.zeros((LANES - 2 * nh,), F32)]).reshape(1, LANES)
    arow = jnp.broadcast_to(jnp.concatenate([zpad, a_log]).reshape(GATE_LANES, 1), (GATE_LANES, CHUNK))
    dtrow = jnp.broadcast_to(jnp.concatenate([zpad, dt_bias]).reshape(GATE_LANES, 1), (GATE_LANES, CHUNK))
    kern = functools.partial(_gdn_kernel, ts=ts)
    const = lambda b: (0, 0)
    return pl.pallas_call(
        kern,
        grid=(bsz,),
        in_specs=[
            pl.BlockSpec((None, ts, DN_W), lambda b: (b, 0, COLBLK_DQ)),
            pl.BlockSpec((None, ts, DN_W), lambda b: (b, 0, COLBLK_DK)),
            pl.BlockSpec((None, ts, DN_W), lambda b: (b, 0, COLBLK_DV)),
            pl.BlockSpec((None, ts, DN_W), lambda b: (b, 0, COLBLK_Z)),
            pl.BlockSpec((None, ts, LANES), lambda b: (b, 0, 0)),
            pl.BlockSpec((None, ts // CHUNK, GATE_LANES, CHUNK), lambda b: (b, 0, 0, 0)),
            pl.BlockSpec((CONV_K, 3 * DN_W), const),
            pl.BlockSpec((1, LANES), const),
            pl.BlockSpec((1, LANES), const),
            pl.BlockSpec((GATE_LANES, CHUNK), const),
            pl.BlockSpec((GATE_LANES, CHUNK), const),
            pl.BlockSpec((1, DN_HEAD_DIM), const),
        ],
        out_specs=pl.BlockSpec((None, ts, DN_W), lambda b: (b, 0, 0)),
        out_shape=jax.ShapeDtypeStruct((bsz, s, DN_W), BF16),
        scratch_shapes=[
            pltpu.VMEM((DN_HEADS, DN_HEAD_DIM, DN_HEAD_DIM), F32),
            pltpu.VMEM((2 * DN_HEADS, CHUNK, DN_HEAD_DIM), F32),
            pltpu.VMEM((2 * DN_HEADS, CHUNK, DN_HEAD_DIM), BF16),
            pltpu.VMEM((3 * DN_HEADS, CHUNK, DN_HEAD_DIM), BF16),
            pltpu.VMEM((3 * DN_HEADS, CHUNK, DN_HEAD_DIM), BF16),
            pltpu.VMEM((3 * DN_HEADS, CHUNK, CHUNK), BF16),
            pltpu.VMEM((3 * DN_HEADS, SUBLANES, DN_HEAD_DIM), F32),
            pltpu.VMEM((2 * DN_HEADS, CHUNK, CHUNK), F32),
            pltpu.VMEM((2 * DN_HEADS, CHUNK, 2 * DN_HEAD_DIM), BF16),
        ],
        compiler_params=_params("parallel", vmem_limit=GDN_VMEM_LIMIT),
        name="gated_deltanet",
    )(proj, proj, proj, proj, gate, gate_t, conv_w, acol, dtcol, arow, dtrow, norm_w.reshape(1, DN_HEAD_DIM))


def _mix_kernel(att_ref, gdn_ref, ga_ref, gb_ref, x_ref, mod_ref, woa_ref, wob_ref, wout_ref, g_ref, b_ref, o_ref):
    y_a = jnp.dot(att_ref[...], woa_ref[...], preferred_element_type=F32)
    y_b = jnp.dot(gdn_ref[...], wob_ref[...], preferred_element_type=F32)
    merged = jax.nn.sigmoid(ga_ref[...].astype(F32)) * y_a + jax.nn.sigmoid(gb_ref[...].astype(F32)) * y_b
    mixed = jnp.dot(merged.astype(BF16), wout_ref[...], preferred_element_type=F32)
    o_ref[...] = _layer_norm(ALPHA * x_ref[...] + (1.0 + mod_ref[2:3, :]) * mixed, g_ref[...], b_ref[...])


def _mix(att, gdn, proj, x, mod_l, w_oa, w_ob, w_out, ln_g, ln_b):
    bsz, s, d = x.shape
    tm = min(512, s)
    row = lambda b, i: (b, i, 0)
    const = lambda b, i: (0, 0)
    return pl.pallas_call(
        _mix_kernel,
        grid=(bsz, s // tm),
        in_specs=[
            pl.BlockSpec((None, tm, d), row),
            pl.BlockSpec((None, tm, d), row),
            pl.BlockSpec((None, tm, d), lambda b, i: (b, i, COLBLK_GA)),
            pl.BlockSpec((None, tm, d), lambda b, i: (b, i, COLBLK_GB)),
            pl.BlockSpec((None, tm, d), row),
            pl.BlockSpec((None, 6, d), lambda b, i: (b, 0, 0)),
            pl.BlockSpec((d, d), const),
            pl.BlockSpec((d, d), const),
            pl.BlockSpec((d, d), const),
            pl.BlockSpec((1, d), const),
            pl.BlockSpec((1, d), const),
        ],
        out_specs=pl.BlockSpec((None, tm, d), row),
        out_shape=jax.ShapeDtypeStruct((bsz, s, d), F32),
        compiler_params=_params("parallel", "parallel"),
        name="mix_out_ln",
    )(att, gdn, proj, proj, x, mod_l, w_oa, w_ob, w_out, ln_g.reshape(1, d), ln_b.reshape(1, d))


def _ffn_kernel(x_ref, mod_ref, w1_ref, b1_ref, w2_ref, b2_ref, g_ref, b_ref, o_ref, u_scr, acc):
    f = pl.program_id(2)

    @pl.when(f == 0)
    def _():
        u_scr[...] = (x_ref[...] * (1.0 + mod_ref[4:5, :]) + mod_ref[3:4, :]).astype(BF16)
        acc[...] = jnp.zeros_like(acc)

    h = jnp.dot(u_scr[...], w1_ref[...], preferred_element_type=F32) + b1_ref[...]
    h = jnp.square(jnp.maximum(h, 0.0))
    acc[...] += jnp.dot(h.astype(BF16), w2_ref[...], preferred_element_type=F32)

    @pl.when(f == pl.num_programs(2) - 1)
    def _():
        y = acc[...] + b2_ref[...]
        o_ref[...] = _layer_norm(ALPHA * x_ref[...] + (1.0 + mod_ref[5:6, :]) * y, g_ref[...], b_ref[...])


def _ffn(x, mod_l, w1, b1, w2, b2, ln_g, ln_b):
    bsz, s, d = x.shape
    dff = w1.shape[1]
    tm = min(1024, s)
    tf = 1024
    row = lambda b, i, f: (b, i, 0)
    const = lambda b, i, f: (0, 0)
    return pl.pallas_call(
        _ffn_kernel,
        grid=(bsz, s // tm, dff // tf),
        in_specs=[
            pl.BlockSpec((None, tm, d), row),
            pl.BlockSpec((None, 6, d), lambda b, i, f: (b, 0, 0)),
            pl.BlockSpec((d, tf), lambda b, i, f: (0, f)),
            pl.BlockSpec((1, tf), lambda b, i, f: (0, f)),
            pl.BlockSpec((tf, d), lambda b, i, f: (f, 0)),
            pl.BlockSpec((1, d), const),
            pl.BlockSpec((1, d), const),
            pl.BlockSpec((1, d), const),
        ],
        out_specs=pl.BlockSpec((None, tm, d), row),
        out_shape=jax.ShapeDtypeStruct((bsz, s, d), F32),
        scratch_shapes=[pltpu.VMEM((tm, d), BF16), pltpu.VMEM((tm, d), F32)],
        compiler_params=_params("parallel", "parallel", "arbitrary"),
        name="ffn_ln",
    )(x, mod_l, w1, b1.reshape(1, dff), w2, b2.reshape(1, d), ln_g.reshape(1, d), ln_b.reshape(1, d))


def kernel(x, c, w_ada, b_ada, w_in, conv_w, a_log, dt_bias, sinks, dn_norm_w, w_oa, w_ob, w_out,
           ln1_g, ln1_b, w_ff1, b_ff1, w_ff2, b_ff2, ln2_g, ln2_b):
    depth = w_ada.shape[0]
    bsz, s, d = x.shape
    mod = _ada_mod(c, w_ada, b_ada).reshape(depth, bsz, 6, d)
    gate_lo = ATT_Q_W + 2 * ATT_KV_W + 4 * DN_W
    gate_hi = gate_lo + GATE_LANES
    w_gate_f32 = w_in[:, :, gate_lo:gate_hi]
    w_gate_all = jnp.pad(w_gate_f32, ((0, 0), (0, 0), (0, LANES - GATE_LANES))).astype(BF16)
    w_gate_t_all = jnp.swapaxes(w_gate_f32, 1, 2).astype(BF16)
    for l in range(depth):
        wl = w_in[l]
        w_main = jnp.concatenate(
            [wl[:, :ATT_Q_W] * (ATT_HEAD_DIM ** -0.5), wl[:, ATT_Q_W + 2 * ATT_KV_W:gate_lo], wl[:, gate_hi:],
             wl[:, ATT_Q_W:ATT_Q_W + 2 * ATT_KV_W]], axis=1).astype(BF16)
        proj, gate, gate_t = _in_proj(x, mod[l], w_main, w_gate_all[l], w_gate_t_all[l])
        att = _attention(proj, sinks[l])
        gdn = _gated_deltanet(proj, gate, gate_t, conv_w[l], a_log[l], dt_bias[l], dn_norm_w[l])
        x = _mix(att, gdn, proj, x, mod[l], w_oa[l].astype(BF16), w_ob[l].astype(BF16), w_out[l].astype(BF16),
                 ln1_g[l], ln1_b[l])
        x = _ffn(x, mod[l], w_ff1[l].astype(BF16), b_ff1[l], w_ff2[l].astype(BF16), b_ff2[l], ln2_g[l], ln2_b[l])
    return x
```
